```python
import math
import jax, jax.numpy as jnp
from jax import lax
import numpy as np

D_MODEL = 4096
BATCH = 8
SEQ = 2048
DEPTH = 1
DEC_BATCH = 4
DEC_SEQ = 4096
PAST_LEN = 128

HEAD_DIM = 128
N_ATTN_HEADS = D_MODEL // 2 // HEAD_DIM
N_KV_HEADS = N_ATTN_HEADS // 4
ATTN_WIDTH = N_ATTN_HEADS * HEAD_DIM
KV_WIDTH = N_KV_HEADS * HEAD_DIM
WINDOW = 128
BLOCK = 128
GLA_DV = 256
N_GLA_HEADS = D_MODEL // 2 // GLA_DV
GLA_DK = GLA_DV // 2
GLA_K_WIDTH = N_GLA_HEADS * GLA_DK
GLA_V_WIDTH = N_GLA_HEADS * GLA_DV
GATE_RANK = 16
GATE_TEMP = 16.0
CHUNK = 32
MIX_WIDTH = ATTN_WIDTH + GLA_V_WIDTH
IN_WIDTH = ATTN_WIDTH + 2 * KV_WIDTH + 2 * GLA_K_WIDTH + 2 * GLA_V_WIDTH + 2 * GATE_RANK
D_FF = 4 * D_MODEL
N_BUCKETS = 32
MAX_DISTANCE = 128
EPS = 1e-6
NEG_INF = -1e30

kernel_name = "hymba_style_window_gqa_gla_encoder"


def _rms(x, g):
    xf = x.astype(jnp.float32)
    y = xf * lax.rsqrt(jnp.mean(xf * xf, axis=-1, keepdims=True) + EPS)
    return (y * g.astype(jnp.float32)).astype(x.dtype)


def _t5_bucket(rel):
    half = N_BUCKETS // 2
    max_exact = half // 2
    ret = jnp.where(rel > 0, half, 0)
    n = jnp.abs(rel)
    nf = jnp.maximum(n, 1).astype(jnp.float32)
    large = max_exact + (jnp.log(nf / max_exact) / math.log(MAX_DISTANCE / max_exact)
                         * (half - max_exact)).astype(jnp.int32)
    large = jnp.minimum(large, half - 1)
    return ret + jnp.where(n < max_exact, n, large)


def _split_proj(proj):
    sizes = [ATTN_WIDTH, KV_WIDTH, KV_WIDTH, GLA_K_WIDTH, GLA_K_WIDTH,
             GLA_V_WIDTH, GLA_V_WIDTH, GATE_RANK, GATE_RANK]
    idx = np.cumsum(sizes)[:-1].tolist()
    return jnp.split(proj, idx, axis=-1)


def _window_attention(q, k, v, sink, rel_bias):
    B, S, _ = q.shape
    nb = S // BLOCK
    G = N_ATTN_HEADS // N_KV_HEADS
    qb = q.reshape(B, nb, BLOCK, N_KV_HEADS, G, HEAD_DIM)

    def windows(t):
        t = t.reshape(B, S, N_KV_HEADS, HEAD_DIM)
        tp = jnp.pad(t, ((0, 0), (BLOCK, BLOCK), (0, 0), (0, 0)))
        tp = tp.reshape(B, nb + 2, BLOCK, N_KV_HEADS, HEAD_DIM)
        return jnp.concatenate([tp[:, :-2], tp[:, 1:-1], tp[:, 2:]], axis=2)

    kw, vw = windows(k), windows(v)
    s = jnp.einsum('bnqkgd,bnjkd->bnkgqj', qb, kw).astype(jnp.float32) * (HEAD_DIM ** -0.5)
    qi = jnp.arange(BLOCK)[:, None]
    kj = jnp.arange(3 * BLOCK)[None, :]
    rel = kj - BLOCK - qi
    bias = rel_bias[_t5_bucket(rel)].astype(jnp.float32)
    bias = bias.transpose(2, 0, 1).reshape(N_KV_HEADS, G, BLOCK, 3 * BLOCK)
    kpos = (jnp.arange(nb)[:, None] - 1) * BLOCK + jnp.arange(3 * BLOCK)[None, :]
    valid = (jnp.abs(rel) <= WINDOW)[None] & ((kpos >= 0) & (kpos < S))[:, None, :]
    s = jnp.where(valid[None, :, None, None], s + bias, NEG_INF)
    sink_l = jnp.broadcast_to(sink.astype(jnp.float32).reshape(N_KV_HEADS, G, 1, 1), s.shape[:-1] + (1,))
    p = jax.nn.softmax(jnp.concatenate([s, sink_l], axis=-1), axis=-1)[..., :-1]
    o = jnp.einsum('bnkgqj,bnjkd->bnqkgd', p.astype(v.dtype), vw)
    return o.reshape(B, S, ATTN_WIDTH)


def _gla_direction(q, k, v, logg):
    B, H, S, dk = q.shape
    dv = v.shape[-1]
    nc = S // CHUNK
    qc = q.reshape(B, H, nc, CHUNK, dk)
    kc = k.reshape(B, H, nc, CHUNK, dk)
    vc = v.reshape(B, H, nc, CHUNK, dv)
    gc = logg.reshape(B, H, nc, CHUNK, dk)
    b = jnp.cumsum(gc, axis=3)
    b_last = b[:, :, :, -1:, :]
    q_t = qc * jnp.exp(b)
    k_t = kc * jnp.exp(-b)
    mask = jnp.tril(jnp.ones((CHUNK, CHUNK), dtype=bool))
    A = jnp.where(mask, jnp.einsum('bhnck,bhnsk->bhncs', q_t, k_t), 0.0)
    o_intra = jnp.einsum('bhncs,bhnsv->bhncv', A, vc)
    k_dec = kc * jnp.exp(b_last - b)
    decay = jnp.exp(b_last[:, :, :, 0, :])

    def step(state, inp):
        q_n, k_n, v_n, d_n = inp
        o_n = jnp.einsum('bhck,bhkv->bhcv', q_n, state)
        state = state * d_n[..., None] + jnp.einsum('bhck,bhcv->bhkv', k_n, v_n)
        return state, o_n

    xs = (jnp.moveaxis(q_t, 2, 0), jnp.moveaxis(k_dec, 2, 0),
          jnp.moveaxis(vc, 2, 0), jnp.moveaxis(decay, 2, 0))
    _, o_inter = lax.scan(step, jnp.zeros((B, H, dk, dv), jnp.float32), xs)
    return (o_intra + jnp.moveaxis(o_inter, 0, 2)).reshape(B, H, S, dv)


def _gla_mixer(qg, kg, vg, rg, af, ab, wa_f, ba_f, wa_b, ba_b, norm_g):
    B, S, _ = qg.shape

    def heads(t, d):
        return t.reshape(B, S, N_GLA_HEADS, d).transpose(0, 2, 1, 3).astype(jnp.float32)

    q = heads(qg, GLA_DK) * (GLA_DK ** -0.5)
    k = heads(kg, GLA_DK)
    v = heads(vg, GLA_DV)
    lg_f = heads(jax.nn.log_sigmoid((af @ wa_f + ba_f).astype(jnp.float32)) / GATE_TEMP, GLA_DK)
    lg_b = heads(jax.nn.log_sigmoid((ab @ wa_b + ba_b).astype(jnp.float32)) / GATE_TEMP, GLA_DK)
    o_f = _gla_direction(q, k, v, lg_f)
    flip = lambda t: jnp.flip(t, axis=2)
    o_b = flip(_gla_direction(flip(q), flip(k), flip(v), flip(lg_b)))
    o = o_f + o_b
    o = o * lax.rsqrt(jnp.mean(o * o, axis=-1, keepdims=True) + EPS) * norm_g.astype(jnp.float32)
    o = o.transpose(0, 2, 1, 3).reshape(B, S, GLA_V_WIDTH).astype(rg.dtype)
    return o * jax.nn.silu(rg)


def _layer(x, c, w_ada, b_ada, norm1_g, w_in, gla_wa_fwd, gla_ba_fwd, gla_wa_bwd, gla_ba_bwd,
           gla_norm_g, attn_sink, rel_bias, w_out, norm2_g, w_mlp_in, w_mlp_out):
    mod = jax.nn.silu(c) @ w_ada + b_ada
    sh1, sc1, g1, sh2, sc2, g2 = jnp.split(mod[:, None, :], 6, axis=-1)
    h = _rms(x, norm1_g) * (1.0 + sc1) + sh1
    qa, ka, va, qg, kg, vg, rg, af, ab = _split_proj(h @ w_in)
    attn = _window_attention(qa, ka, va, attn_sink, rel_bias)
    gla = _gla_mixer(qg, kg, vg, rg, af, ab, gla_wa_fwd, gla_ba_fwd, gla_wa_bwd, gla_ba_bwd, gla_norm_g)
    x = x + g1 * (jnp.concatenate([attn, gla], axis=-1) @ w_out)
    h = _rms(x, norm2_g) * (1.0 + sc2) + sh2
    x = x + g2 * (jnp.square(jax.nn.relu(h @ w_mlp_in)) @ w_mlp_out)
    return x


def _trunk(x, c, w_ada, b_ada, norm1_g, w_in, gla_wa_fwd, gla_ba_fwd, gla_wa_bwd, gla_ba_bwd,
           gla_norm_g, attn_sink, rel_bias, w_out, norm2_g, w_mlp_in, w_mlp_out, final_g):
    for l in range(DEPTH):
        x = _layer(x, c, w_ada[l], b_ada[l], norm1_g[l], w_in[l], gla_wa_fwd[l], gla_ba_fwd[l],
                   gla_wa_bwd[l], gla_ba_bwd[l], gla_norm_g[l], attn_sink[l], rel_bias,
                   w_out[l], norm2_g[l], w_mlp_in[l], w_mlp_out[l])
    return _rms(x, final_g)


def setup_inputs(seed: int = 0) -> dict:
    key = jax.random.key(seed)
    ks = jax.random.split(key, 24)
    f32 = jnp.float32
    nrm = lambda k, shape, s: jax.random.normal(k, shape, f32) * s
    return {
        "x_prompt": nrm(ks[0], (BATCH, SEQ, D_MODEL), 1.0),
        "x_sample": nrm(ks[1], (DEC_BATCH, DEC_SEQ, D_MODEL), 1.0),
        "c_prompt": nrm(ks[2], (BATCH, D_MODEL), 1.0),
        "c_sample": nrm(ks[3], (DEC_BATCH, D_MODEL), 1.0),
        "w_ada": nrm(ks[4], (DEPTH, D_MODEL, 6 * D_MODEL), D_MODEL ** -0.5),
        "b_ada": nrm(ks[5], (DEPTH, 6 * D_MODEL), 0.01),
        "norm1_g": 1.0 + nrm(ks[6], (DEPTH, D_MODEL), 0.02),
        "w_in": nrm(ks[7], (DEPTH, D_MODEL, IN_WIDTH), D_MODEL ** -0.5),
        "gla_wa_fwd": nrm(ks[8], (DEPTH, GATE_RANK, GLA_K_WIDTH), GATE_RANK ** -0.5),
        "gla_ba_fwd": nrm(ks[9], (DEPTH, GLA_K_WIDTH), 0.1),
        "gla_wa_bwd": nrm(ks[10], (DEPTH, GATE_RANK, GLA_K_WIDTH), GATE_RANK ** -0.5),
        "gla_ba_bwd": nrm(ks[11], (DEPTH, GLA_K_WIDTH), 0.1),
        "gla_norm_g": 1.0 + nrm(ks[12], (DEPTH, GLA_DV), 0.02),
        "attn_sink": nrm(ks[13], (DEPTH, N_ATTN_HEADS), 1.0),
        "rel_bias": nrm(ks[14], (N_BUCKETS, N_ATTN_HEADS), 0.5),
        "w_out": nrm(ks[15], (DEPTH, MIX_WIDTH, D_MODEL), MIX_WIDTH ** -0.5),
        "norm2_g": 1.0 + nrm(ks[16], (DEPTH, D_MODEL), 0.02),
        "w_mlp_in": nrm(ks[17], (DEPTH, D_MODEL, D_FF), D_MODEL ** -0.5),
        "w_mlp_out": nrm(ks[18], (DEPTH, D_FF, D_MODEL), D_FF ** -0.5),
        "final_g": 1.0 + nrm(ks[19], (D_MODEL,), 0.02),
    }


def reference(x_prompt, x_sample, c_prompt, c_sample, w_ada, b_ada, norm1_g, w_in,
              gla_wa_fwd, gla_ba_fwd, gla_wa_bwd, gla_ba_bwd, gla_norm_g, attn_sink, rel_bias,
              w_out, norm2_g, w_mlp_in, w_mlp_out, final_g):
    y_prompt = _trunk(x_prompt, c_prompt, w_ada, b_ada, norm1_g, w_in, gla_wa_fwd, gla_ba_fwd,
                      gla_wa_bwd, gla_ba_bwd, gla_norm_g, attn_sink, rel_bias, w_out, norm2_g,
                      w_mlp_in, w_mlp_out, final_g)
    y_sample = _trunk(x_sample, c_sample, w_ada, b_ada, norm1_g, w_in, gla_wa_fwd, gla_ba_fwd,
                      gla_wa_bwd, gla_ba_bwd, gla_norm_g, attn_sink, rel_bias, w_out, norm2_g,
                      w_mlp_in, w_mlp_out, final_g)
    return (y_prompt, y_sample)
```

```python
import functools
import math

import numpy as np
import jax
import jax.numpy as jnp
from jax import lax
from jax.experimental import pallas as pl
from jax.experimental.pallas import tpu as pltpu

F32 = jnp.float32
BF16 = jnp.bfloat16

HEAD_DIM = 128
N_ATTN_HEADS = 16
N_KV_HEADS = 4
GQA_GROUP = N_ATTN_HEADS // N_KV_HEADS
ATTN_WIDTH = N_ATTN_HEADS * HEAD_DIM
KV_WIDTH = N_KV_HEADS * HEAD_DIM
WINDOW = 128
BLOCK = 128
GLA_DV = 256
GLA_DK = 128
N_GLA_HEADS = 8
GLA_K_WIDTH = N_GLA_HEADS * GLA_DK
GLA_V_WIDTH = N_GLA_HEADS * GLA_DV
GATE_RANK = 16
GATE_TEMP = 16.0
N_BUCKETS = 32
MAX_DISTANCE = 128
EPS = 1e-6
NEG_INF = -1e30

OFF_QA = 0
OFF_KA = OFF_QA + ATTN_WIDTH
OFF_VA = OFF_KA + KV_WIDTH
OFF_QG = OFF_VA + KV_WIDTH
OFF_KG = OFF_QG + GLA_K_WIDTH
OFF_VG = OFF_KG + GLA_K_WIDTH
OFF_RG = OFF_VG + GLA_V_WIDTH
OFF_GATE = OFF_RG + GLA_V_WIDTH
PROJ_WIDTH = OFF_GATE
GATE_PAD = 128

GLA_CHUNK = 64
GLA_BLOCK = 256
NORM_ROWS = 32

VMEM_LIMIT = 60 * 1024 * 1024


def _cparams(n_axes, vmem=VMEM_LIMIT):
    return pltpu.CompilerParams(dimension_semantics=("arbitrary",) * n_axes,
                                vmem_limit_bytes=vmem)


def _dot(a, b):
    return jnp.dot(a, b, preferred_element_type=F32)


def _dot_nt(a, b):
    return lax.dot_general(a, b, (((1,), (1,)), ((), ())), preferred_element_type=F32)


def _dot_tn(a, b):
    return lax.dot_general(a, b, (((0,), (0,)), ((), ())), preferred_element_type=F32)


def _silu(x):
    return x * (1.0 / (1.0 + jnp.exp(-x)))


def _row_loop(n_rows, rows_per, body):
    def step(i, carry):
        body(pl.multiple_of(i * rows_per, rows_per))
        return carry
    lax.fori_loop(0, n_rows // rows_per, step, 0)


def _ada_kernel(c_ref, w_ref, b_ref, o_ref):
    a = _silu(c_ref[...]).astype(BF16)
    o_ref[...] = _dot(a, w_ref[...].astype(BF16)) + b_ref[...]


def _ada(c_pad, w_ada, b_ada, tn=512):
    rows, d = c_pad.shape
    n = w_ada.shape[1]
    return pl.pallas_call(
        _ada_kernel,
        grid=(n // tn,),
        in_specs=[pl.BlockSpec((rows, d), lambda j: (0, 0)),
                  pl.BlockSpec((d, tn), lambda j: (0, j)),
                  pl.BlockSpec((1, tn), lambda j: (0, j))],
        out_specs=pl.BlockSpec((rows, tn), lambda j: (0, j)),
        out_shape=jax.ShapeDtypeStruct((rows, n), F32),
        compiler_params=_cparams(1),
        name="ada",
    )(c_pad, w_ada, b_ada.reshape(1, n))


def _modulated_rms_rows(x_ref, h_ref, g, sc, sh, n_rows):
    inv_d = 1.0 / x_ref.shape[-1]

    def body(r0):
        x = x_ref[pl.ds(r0, NORM_ROWS), :]
        ms = jnp.sum(x * x, axis=-1, keepdims=True) * inv_d
        y = x * lax.rsqrt(ms + EPS) * g
        h_ref[pl.ds(r0, NORM_ROWS), :] = (y * (1.0 + sc) + sh).astype(BF16)

    _row_loop(n_rows, NORM_ROWS, body)


def _inproj_kernel(x_ref, sh_ref, sc_ref, g_ref, w_ref, wg_ref, o_ref, og_ref, h_ref):
    @pl.when(pl.program_id(1) == 0)
    def _():
        _modulated_rms_rows(x_ref, h_ref, g_ref[...], sc_ref[0], sh_ref[0], x_ref.shape[0])
        og_ref[...] = _dot(h_ref[...], wg_ref[...])

    o_ref[...] = _dot(h_ref[...], w_ref[...]).astype(o_ref.dtype)


def _in_proj(x2, mod3, b0, seq, norm_g, w_bf, wg_bf, tm=512, tn=1024):
    m, d = x2.shape
    n = w_bf.shape[1]
    mod_spec = lambda k: pl.BlockSpec((1, 1, d), lambda i, j: ((b0 + (i * tm) // seq) * 6 + k, 0, 0))
    return pl.pallas_call(
        _inproj_kernel,
        grid=(m // tm, n // tn),
        in_specs=[pl.BlockSpec((tm, d), lambda i, j: (i, 0)),
                  mod_spec(0), mod_spec(1),
                  pl.BlockSpec((1, d), lambda i, j: (0, 0)),
                  pl.BlockSpec((d, tn), lambda i, j: (0, j)),
                  pl.BlockSpec((d, GATE_PAD), lambda i, j: (0, 0))],
        out_specs=[pl.BlockSpec((tm, tn), lambda i, j: (i, j)),
                   pl.BlockSpec((tm, GATE_PAD), lambda i, j: (i, 0))],
        out_shape=[jax.ShapeDtypeStruct((m, n), BF16),
                   jax.ShapeDtypeStruct((m, GATE_PAD), F32)],
        scratch_shapes=[pltpu.VMEM((tm, d), BF16)],
        compiler_params=_cparams(2),
        name="in_proj",
    )(x2, mod3, mod3, norm_g, w_bf, wg_bf)


def _t5_bucket_table():
    half = N_BUCKETS // 2
    max_exact = half // 2
    qi = np.arange(BLOCK)[:, None]
    kj = np.arange(3 * BLOCK)[None, :]
    rel = kj - BLOCK - qi
    n = np.abs(rel)
    nf = np.maximum(n, 1).astype(np.float32)
    large = max_exact + (np.log(nf / max_exact) / math.log(MAX_DISTANCE / max_exact)
                         * (half - max_exact)).astype(np.int32)
    large = np.minimum(large, half - 1)
    return (np.where(rel > 0, half, 0) + np.where(n < max_exact, n, large)).astype(np.int32)


def _attn_kernel(rb_ref, sink_ref, bucket_ref, q_ref, k_ref, v_ref, o_ref, tbl_ref, *, nb):
    b = pl.program_id(0)
    n = pl.program_id(1)

    @pl.when((b == 0) & (n == 0))
    def _():
        qi = lax.broadcasted_iota(jnp.int32, (BLOCK, 3 * BLOCK), 0)
        kj = lax.broadcasted_iota(jnp.int32, (BLOCK, 3 * BLOCK), 1)
        in_band = jnp.abs(kj - BLOCK - qi) <= WINDOW
        bucket = bucket_ref[...]

        def per_head(h, carry):
            def per_bucket(bk, acc):
                return jnp.where(bucket == bk, rb_ref[bk, h], acc)
            acc = lax.fori_loop(0, N_BUCKETS, per_bucket, jnp.zeros((BLOCK, 3 * BLOCK), F32))
            tbl_ref[h] = jnp.where(in_band, acc, NEG_INF)
            return carry

        lax.fori_loop(0, N_ATTN_HEADS, per_head, 0)

    r_prev = pl.multiple_of(jnp.maximum(n - 1, 0) * BLOCK, BLOCK)
    r_cur = pl.multiple_of(n * BLOCK, BLOCK)
    r_next = pl.multiple_of(jnp.minimum(n + 1, nb - 1) * BLOCK, BLOCK)
    col = lax.broadcasted_iota(jnp.int32, (1, 3 * BLOCK), 1)
    pen = jnp.where((col < BLOCK) & (n == 0), NEG_INF, 0.0)
    pen = jnp.where((col >= 2 * BLOCK) & (n == nb - 1), NEG_INF, pen)
    scale = HEAD_DIM ** -0.5

    for g in range(N_KV_HEADS):
        kc = slice(g * HEAD_DIM, (g + 1) * HEAD_DIM)
        k3 = jnp.concatenate([k_ref[0, pl.ds(r_prev, BLOCK), kc],
                              k_ref[0, pl.ds(r_cur, BLOCK), kc],
                              k_ref[0, pl.ds(r_next, BLOCK), kc]], axis=0)
        v3 = jnp.concatenate([v_ref[0, pl.ds(r_prev, BLOCK), kc],
                              v_ref[0, pl.ds(r_cur, BLOCK), kc],
                              v_ref[0, pl.ds(r_next, BLOCK), kc]], axis=0)
        heads = [g * GQA_GROUP + j for j in range(GQA_GROUP)]
        qs = jnp.concatenate([q_ref[0, :, h * HEAD_DIM:(h + 1) * HEAD_DIM] for h in heads], axis=0)
        s = _dot_nt(qs, k3)
        ps, inv_dens = [], []
        for j, h in enumerate(heads):
            sj = s[j * BLOCK:(j + 1) * BLOCK] * scale + tbl_ref[h] + pen
            sink = sink_ref[0, h]
            m = jnp.maximum(jnp.max(sj, axis=-1, keepdims=True), sink)
            p = jnp.exp(sj - m)
            den = jnp.sum(p, axis=-1, keepdims=True) + jnp.exp(sink - m)
            ps.append(p.astype(BF16))
            inv_dens.append(1.0 / den)
        o = _dot(jnp.concatenate(ps, axis=0), v3)
        for j, h in enumerate(heads):
            o_ref[0, :, h * HEAD_DIM:(h + 1) * HEAD_DIM] = (
                o[j * BLOCK:(j + 1) * BLOCK] * inv_dens[j]).astype(o_ref.dtype)


def _attention(proj3, rel_bias, sink, bucket):
    bsz, seq, _ = proj3.shape
    nb = seq // BLOCK
    smem = pl.BlockSpec(memory_space=pltpu.SMEM)
    return pl.pallas_call(
        functools.partial(_attn_kernel, nb=nb),
        grid=(bsz, nb),
        in_specs=[smem, smem,
                  pl.BlockSpec((BLOCK, 3 * BLOCK), lambda b, n: (0, 0)),
                  pl.BlockSpec((1, BLOCK, ATTN_WIDTH), lambda b, n: (b, n, OFF_QA // ATTN_WIDTH)),
                  pl.BlockSpec((1, seq, KV_WIDTH), lambda b, n: (b, 0, OFF_KA // KV_WIDTH)),
                  pl.BlockSpec((1, seq, KV_WIDTH), lambda b, n: (b, 0, OFF_VA // KV_WIDTH))],
        out_specs=pl.BlockSpec((1, BLOCK, ATTN_WIDTH), lambda b, n: (b, n, 0)),
        out_shape=jax.ShapeDtypeStruct((bsz, seq, ATTN_WIDTH), BF16),
        scratch_shapes=[pltpu.VMEM((N_ATTN_HEADS, BLOCK, 3 * BLOCK), F32)],
        compiler_params=_cparams(2),
        name="window_attn",
    )(rel_bias, sink, bucket, proj3, proj3, proj3)


def _gla_tri_mats():
    t = np.arange(GLA_BLOCK)
    same = (t[:, None] // GLA_CHUNK) == (t[None, :] // GLA_CHUNK)
    fwd = same & (t[None, :] <= t[:, None])
    bwd = same & (t[None, :] >= t[:, None])
    return np.stack([fwd, bwd]).astype(np.float32)


def _gla_kernel(tri_ref, q_ref, k_ref, v_ref, r_ref, gt_ref, wa_ref, ba_ref, ng_ref, o_ref,
                qp_ref, kp_ref, qt_ref, kd_ref, dec_ref, st_ref, acc_ref, *, seq):
    c = GLA_CHUNK
    nc = seq // c
    scale = GLA_DK ** -0.5
    anchor = (c // 2 - 1, c // 2)
    last = (c - 1, 0)

    def pre(r0):
        gtb = gt_ref[0, pl.ds(r0, GLA_BLOCK), :].astype(BF16)
        for d in range(2):
            z = _dot(gtb, wa_ref[d]) + ba_ref[d]
            lg = (jnp.minimum(z, 0.0) - jnp.log(1.0 + jnp.exp(-jnp.abs(z)))) * (1.0 / GATE_TEMP)
            hi = lg.astype(BF16)
            lo = (lg - hi.astype(F32)).astype(BF16)
            tri = tri_ref[d]
            bcum = _dot(tri, hi) + _dot(tri, lo)
            for ci in range(GLA_BLOCK // c):
                rows = pl.ds(r0 + ci * c, c)
                bc = bcum[ci * c:(ci + 1) * c]
                b_a = bc[anchor[d]:anchor[d] + 1]
                b_l = bc[last[d]:last[d] + 1]
                qs = q_ref[0, rows, :].astype(F32) * scale
                kf = k_ref[0, rows, :].astype(F32)
                qp_ref[d, rows, :] = (qs * jnp.exp(bc - b_a)).astype(BF16)
                kp_ref[d, rows, :] = (kf * jnp.exp(b_a - bc)).astype(BF16)
                qt_ref[d, rows, :] = (qs * jnp.exp(bc)).astype(BF16)
                kd_ref[d, rows, :] = (kf * jnp.exp(b_l - bc)).astype(BF16)
                dec_ref[d, pl.ds((r0 + ci * c) // c, 1), :] = jnp.exp(b_l)

    _row_loop(seq, GLA_BLOCK, pre)

    st_ref[...] = jnp.zeros_like(st_ref)
    ti = lax.broadcasted_iota(jnp.int32, (c, c), 0)
    si = lax.broadcasted_iota(jnp.int32, (c, c), 1)
    masks = (si <= ti, si >= ti)

    def step(i, carry):
        for d, ch in ((0, i), (1, nc - 1 - i)):
            rows = pl.ds(pl.multiple_of(ch * c, c), c)
            a = _dot_nt(qp_ref[d, rows, :], kp_ref[d, rows, :])
            a = jnp.where(masks[d], a, 0.0).astype(BF16)
            v = v_ref[0, rows, :]
            st = st_ref[d]
            acc_ref[d, rows, :] = _dot(a, v) + _dot_nt(qt_ref[d, rows, :], st.astype(BF16))
            st_ref[d] = st * dec_ref[d, pl.ds(ch, 1), :] + _dot_tn(v, kd_ref[d, rows, :])
        return carry

    lax.fori_loop(0, nc, step, 0)

    ng = ng_ref[...]

    def post(r0):
        rows = pl.ds(r0, GLA_BLOCK)
        o = acc_ref[0, rows, :] + acc_ref[1, rows, :]
        ms = jnp.sum(o * o, axis=-1, keepdims=True) * (1.0 / GLA_DV)
        on = o * lax.rsqrt(ms + EPS) * ng
        o_ref[0, rows, :] = (on * _silu(r_ref[0, rows, :].astype(F32))).astype(o_ref.dtype)

    _row_loop(seq, GLA_BLOCK, post)


def _gla(proj3, gates3, tri, wa_pad, ba, norm_g):
    bsz, seq, _ = proj3.shape
    nc = seq // GLA_CHUNK
    whole = lambda shape: pl.BlockSpec(shape, lambda b, h: (0,) * len(shape))
    return pl.pallas_call(
        functools.partial(_gla_kernel, seq=seq),
        grid=(bsz, N_GLA_HEADS),
        in_specs=[whole((2, GLA_BLOCK, GLA_BLOCK)),
                  pl.BlockSpec((1, seq, GLA_DK), lambda b, h: (b, 0, OFF_QG // GLA_DK + h)),
                  pl.BlockSpec((1, seq, GLA_DK), lambda b, h: (b, 0, OFF_KG // GLA_DK + h)),
                  pl.BlockSpec((1, seq, GLA_DV), lambda b, h: (b, 0, OFF_VG // GLA_DV + h)),
                  pl.BlockSpec((1, seq, GLA_DV), lambda b, h: (b, 0, OFF_RG // GLA_DV + h)),
                  pl.BlockSpec((1, seq, GATE_PAD), lambda b, h: (b, 0, 0)),
                  pl.BlockSpec((2, GATE_PAD, GLA_DK), lambda b, h: (0, 0, h)),
                  pl.BlockSpec((2, 1, GLA_DK), lambda b, h: (0, 0, h)),
                  whole((1, GLA_DV))],
        out_specs=pl.BlockSpec((1, seq, GLA_DV), lambda b, h: (b, 0, h)),
        out_shape=jax.ShapeDtypeStruct((bsz, seq, GLA_V_WIDTH), BF16),
        scratch_shapes=[pltpu.VMEM((2, seq, GLA_DK), BF16),
                        pltpu.VMEM((2, seq, GLA_DK), BF16),
                        pltpu.VMEM((2, seq, GLA_DK), BF16),
                        pltpu.VMEM((2, seq, GLA_DK), BF16),
                        pltpu.VMEM((2, nc, GLA_DK), F32),
                        pltpu.VMEM((2, GLA_DV, GLA_DK), F32),
                        pltpu.VMEM((2, seq, GLA_DV), F32)],
        compiler_params=_cparams(2),
        name="gla",
    )(tri, proj3, proj3, proj3, proj3, gates3, wa_pad, ba, norm_g)


def _outproj_kernel(a_ref, g_ref, wa_ref, wg_ref, x_ref, g1_ref, o_ref):
    acc = _dot(a_ref[...], wa_ref[...]) + _dot(g_ref[...], wg_ref[...])
    o_ref[...] = x_ref[...] + g1_ref[0] * acc


def _out_proj(attn2, gla2, w_bf, x2, mod3, b0, seq, tm=512, tn=1024):
    m, d = x2.shape
    ka = attn2.shape[1]
    kg = gla2.shape[1]
    return pl.pallas_call(
        _outproj_kernel,
        grid=(m // tm, d // tn),
        in_specs=[pl.BlockSpec((tm, ka), lambda i, j: (i, 0)),
                  pl.BlockSpec((tm, kg), lambda i, j: (i, 0)),
                  pl.BlockSpec((ka, tn), lambda i, j: (0, j)),
                  pl.BlockSpec((kg, tn), lambda i, j: (ka // kg, j)),
                  pl.BlockSpec((tm, tn), lambda i, j: (i, j)),
                  pl.BlockSpec((1, 1, tn), lambda i, j: ((b0 + (i * tm) // seq) * 6 + 2, 0, j))],
        out_specs=pl.BlockSpec((tm, tn), lambda i, j: (i, j)),
        out_shape=jax.ShapeDtypeStruct((m, d), F32),
        compiler_params=_cparams(2),
        name="out_proj",
    )(attn2, gla2, w_bf, w_bf, x2, mod3)


def _mlp_kernel(x_ref, sh_ref, sc_ref, g2_ref, ng_ref, fg_ref, w1_ref, w2_ref, o_ref, h_ref, *, tn):
    f = pl.program_id(1)
    tm, d = x_ref.shape

    @pl.when(f == 0)
    def _():
        _modulated_rms_rows(x_ref, h_ref, ng_ref[...], sc_ref[0], sh_ref[0], tm)
        o_ref[...] = jnp.zeros_like(o_ref)

    u = _dot(h_ref[...], w1_ref[...])
    u = jnp.square(jnp.maximum(u, 0.0)).astype(BF16)
    for c in range(d // tn):
        cols = slice(c * tn, (c + 1) * tn)
        o_ref[:, cols] += _dot(u, w2_ref[:, cols])

    @pl.when(f == pl.num_programs(1) - 1)
    def _():
        g2 = g2_ref[0]
        fg = fg_ref[...]
        inv_d = 1.0 / d

        def body(r0):
            rows = pl.ds(r0, NORM_ROWS)
            x2 = x_ref[rows, :] + g2 * o_ref[rows, :]
            ms = jnp.sum(x2 * x2, axis=-1, keepdims=True) * inv_d
            o_ref[rows, :] = x2 * lax.rsqrt(ms + EPS) * fg

        _row_loop(tm, NORM_ROWS, body)


def _mlp(x1, mod3, b0, seq, norm_g, final_g, w1_bf, w2_bf, tm=512, tf=512, tn=512):
    m, d = x1.shape
    dff = w1_bf.shape[1]
    mod_spec = lambda k: pl.BlockSpec((1, 1, d), lambda i, f: ((b0 + (i * tm) // seq) * 6 + k, 0, 0))
    return pl.pallas_call(
        functools.partial(_mlp_kernel, tn=tn),
        grid=(m // tm, dff // tf),
        in_specs=[pl.BlockSpec((tm, d), lambda i, f: (i, 0)),
                  mod_spec(3), mod_spec(4), mod_spec(5),
                  pl.BlockSpec((1, d), lambda i, f: (0, 0)),
                  pl.BlockSpec((1, d), lambda i, f: (0, 0)),
                  pl.BlockSpec((d, tf), lambda i, f: (0, f)),
                  pl.BlockSpec((tf, d), lambda i, f: (f, 0))],
        out_specs=pl.BlockSpec((tm, d), lambda i, f: (i, 0)),
        out_shape=jax.ShapeDtypeStruct((m, d), F32),
        scratch_shapes=[pltpu.VMEM((tm, d), BF16)],
        compiler_params=_cparams(2),
        name="mlp",
    )(x1, mod3, mod3, mod3, norm_g, final_g, w1_bf, w2_bf)


def _trunk_group(x, mod3, b0, p):
    bsz, seq, d = x.shape
    x2 = x.reshape(bsz * seq, d)
    proj, gates = _in_proj(x2, mod3, b0, seq, p["norm1_g"], p["w_in"], p["w_gate"])
    proj3 = proj.reshape(bsz, seq, PROJ_WIDTH)
    gates3 = gates.reshape(bsz, seq, GATE_PAD)
    attn = _attention(proj3, p["rel_bias"], p["sink"], p["bucket"])
    gla = _gla(proj3, gates3, p["tri"], p["wa_pad"], p["ba"], p["gla_norm_g"])
    x1 = _out_proj(attn.reshape(bsz * seq, ATTN_WIDTH), gla.reshape(bsz * seq, GLA_V_WIDTH),
                   p["w_out"], x2, mod3, b0, seq)
    y = _mlp(x1, mod3, b0, seq, p["norm2_g"], p["final_g"], p["w_mlp_in"], p["w_mlp_out"])
    return y.reshape(bsz, seq, d)


def kernel(x_prompt, x_sample, c_prompt, c_sample, w_ada, b_ada, norm1_g, w_in, gla_wa_fwd, gla_ba_fwd, gla_wa_bwd, gla_ba_bwd, gla_norm_g, attn_sink, rel_bias, w_out, norm2_g, w_mlp_in, w_mlp_out, final_g):
    assert w_ada.shape[0] == 1, "single-layer trunk"
    d = x_prompt.shape[-1]
    bp, bs = c_prompt.shape[0], c_sample.shape[0]

    rows = -(-(bp + bs) // 16) * 16
    c_pad = jnp.zeros((rows, d), F32).at[:bp].set(c_prompt).at[bp:bp + bs].set(c_sample)
    mod = _ada(c_pad, w_ada[0], b_ada[0])
    mod3 = mod.reshape(rows * 6, 1, d)

    w_in0 = w_in[0]
    wa_pad = jnp.zeros((2, GATE_PAD, GLA_K_WIDTH), F32)
    wa_pad = wa_pad.at[0, :GATE_RANK].set(gla_wa_fwd[0]).at[1, GATE_RANK:2 * GATE_RANK].set(gla_wa_bwd[0])
    p = {
        "norm1_g": norm1_g[0].reshape(1, d),
        "norm2_g": norm2_g[0].reshape(1, d),
        "final_g": final_g.reshape(1, d),
        "w_in": w_in0[:, :PROJ_WIDTH].astype(BF16),
        "w_gate": jnp.pad(w_in0[:, OFF_GATE:], ((0, 0), (0, GATE_PAD - 2 * GATE_RANK))).astype(BF16),
        "w_out": w_out[0].astype(BF16),
        "w_mlp_in": w_mlp_in[0].astype(BF16),
        "w_mlp_out": w_mlp_out[0].astype(BF16),
        "wa_pad": wa_pad.astype(BF16),
        "ba": jnp.stack([gla_ba_fwd[0], gla_ba_bwd[0]]).reshape(2, 1, GLA_K_WIDTH),
        "gla_norm_g": gla_norm_g[0].reshape(1, GLA_DV),
        "rel_bias": rel_bias,
        "sink": attn_sink[0].reshape(1, N_ATTN_HEADS),
        "bucket": jnp.asarray(_t5_bucket_table()),
        "tri": jnp.asarray(_gla_tri_mats(), dtype=BF16),
    }
    y_prompt = _trunk_group(x_prompt, mod3, 0, p)
    y_sample = _trunk_group(x_sample, mod3, bp, p)
    return (y_prompt, y_sample)
```

```python
import functools
import math

import numpy as np
import jax
import jax.numpy as jnp
from jax import lax
from jax.experimental import pallas as pl
from jax.experimental.pallas import tpu as pltpu

F32 = jnp.float32
BF16 = jnp.bfloat16

HEAD_DIM = 128
N_ATTN_HEADS = 16
N_KV_HEADS = 4
GQA_GROUP = N_ATTN_HEADS // N_KV_HEADS
ATTN_WIDTH = N_ATTN_HEADS * HEAD_DIM
KV_WIDTH = N_KV_HEADS * HEAD_DIM
WINDOW = 128
BLOCK = 128
GLA_DV = 256
GLA_DK = 128
N_GLA_HEADS = 8
GLA_K_WIDTH = N_GLA_HEADS * GLA_DK
GLA_V_WIDTH = N_GLA_HEADS * GLA_DV
GATE_RANK = 16
GATE_TEMP = 16.0
N_BUCKETS = 32
MAX_DISTANCE = 128
EPS = 1e-6
NEG_INF = -1e30

OFF_QA = 0
OFF_KA = OFF_QA + ATTN_WIDTH
OFF_VA = OFF_KA + KV_WIDTH
OFF_QG = OFF_VA + KV_WIDTH
OFF_KG = OFF_QG + GLA_K_WIDTH
OFF_VG = OFF_KG + GLA_K_WIDTH
OFF_RG = OFF_VG + GLA_V_WIDTH
OFF_GATE = OFF_RG + GLA_V_WIDTH
PROJ_WIDTH = OFF_GATE
GATE_PAD = 128

GLA_CHUNK = 64
GLA_BLOCK = 256
GLA_SLAB = 64
SUBLANES = 8
LANES = 128
NORM_ROWS = 16
RSQRT_GROUPS = 8

VMEM_LIMIT = 60 * 1024 * 1024


def _cparams(n_axes, vmem=VMEM_LIMIT):
    return pltpu.CompilerParams(dimension_semantics=("arbitrary",) * n_axes,
                                vmem_limit_bytes=vmem)


def _dot(a, b):
    return jnp.dot(a, b, preferred_element_type=F32)


def _dot_nt(a, b):
    return lax.dot_general(a, b, (((1,), (1,)), ((), ())), preferred_element_type=F32)


def _dot_tn(a, b):
    return lax.dot_general(a, b, (((0,), (0,)), ((), ())), preferred_element_type=F32)


def _silu(x):
    return x * (1.0 / (1.0 + jnp.exp(-x)))


def _row_loop(n_rows, rows_per, body):
    def step(i, carry):
        body(pl.multiple_of(i * rows_per, rows_per))
        return carry
    lax.fori_loop(0, n_rows // rows_per, step, 0)


def _ada_kernel(c_ref, w_ref, b_ref, o_ref):
    a = _silu(c_ref[...]).astype(BF16)
    o_ref[...] = _dot(a, w_ref[...].astype(BF16)) + b_ref[...]


def _ada(c_pad, w_ada, b_ada, tn=512):
    rows, d = c_pad.shape
    n = w_ada.shape[1]
    return pl.pallas_call(
        _ada_kernel,
        grid=(n // tn,),
        in_specs=[pl.BlockSpec((rows, d), lambda j: (0, 0)),
                  pl.BlockSpec((d, tn), lambda j: (0, j)),
                  pl.BlockSpec((1, tn), lambda j: (0, j))],
        out_specs=pl.BlockSpec((rows, tn), lambda j: (0, j)),
        out_shape=jax.ShapeDtypeStruct((rows, n), F32),
        compiler_params=_cparams(1),
        name="ada",
    )(c_pad, w_ada, b_ada.reshape(1, n))


def _row_rsqrt(x_ref, r_ref, n_rows, load=None):
    d = x_ref.shape[-1]

    def body(r0):
        for sub in range(RSQRT_GROUPS):
            rows = pl.ds(r0 + sub * SUBLANES, SUBLANES)
            acc = jnp.zeros((SUBLANES, LANES), F32)
            for j in range(d // LANES):
                cols = slice(j * LANES, (j + 1) * LANES)
                x = x_ref[rows, cols] if load is None else load(rows, cols)
                acc = acc + x * x
            ms = jnp.sum(acc, axis=-1, keepdims=True) * (1.0 / d)
            r_ref[rows, :] = jnp.broadcast_to(lax.rsqrt(ms + EPS), (SUBLANES, LANES))

    _row_loop(n_rows, RSQRT_GROUPS * SUBLANES, body)


def _scale_rows(x_ref, o_ref, r_ref, m_ref, n_rows, shift):
    d = x_ref.shape[-1]
    reps = NORM_ROWS // SUBLANES

    def body(r0):
        rows = pl.ds(r0, NORM_ROWS)
        r = r_ref[rows, :]
        for j in range(d // LANES):
            cols = slice(j * LANES, (j + 1) * LANES)
            y = x_ref[rows, cols] * r * jnp.concatenate([m_ref[0, :, cols]] * reps, axis=0)
            if shift:
                y = y + jnp.concatenate([m_ref[1, :, cols]] * reps, axis=0)
            o_ref[rows, cols] = y.astype(o_ref.dtype)

    _row_loop(n_rows, NORM_ROWS, body)


def _modulated_rms_rows(x_ref, h_ref, r_ref, m_ref, g, sc, sh, n_rows):
    d = x_ref.shape[-1]
    m_ref[0] = jnp.broadcast_to(g * (1.0 + sc), (SUBLANES, d))
    m_ref[1] = jnp.broadcast_to(sh, (SUBLANES, d))
    _row_rsqrt(x_ref, r_ref, n_rows)
    _scale_rows(x_ref, h_ref, r_ref, m_ref, n_rows, shift=True)


def _inproj_kernel(x_ref, sh_ref, sc_ref, g_ref, w_ref, wg_ref, o_ref, og_ref, h_ref, r_ref, m_ref):
    @pl.when(pl.program_id(1) == 0)
    def _():
        _modulated_rms_rows(x_ref, h_ref, r_ref, m_ref, g_ref[...], sc_ref[0], sh_ref[0], x_ref.shape[0])
        og_ref[...] = _dot(h_ref[...], wg_ref[...])

    o_ref[...] = _dot(h_ref[...], w_ref[...]).astype(o_ref.dtype)


def _in_proj(x2, mod3, b0, seq, norm_g, w_bf, wg_bf, tm=512, tn=1024):
    m, d = x2.shape
    n = w_bf.shape[1]
    mod_spec = lambda k: pl.BlockSpec((1, 1, d), lambda i, j: ((b0 + (i * tm) // seq) * 6 + k, 0, 0))
    return pl.pallas_call(
        _inproj_kernel,
        grid=(m // tm, n // tn),
        in_specs=[pl.BlockSpec((tm, d), lambda i, j: (i, 0)),
                  mod_spec(0), mod_spec(1),
                  pl.BlockSpec((1, d), lambda i, j: (0, 0)),
                  pl.BlockSpec((d, tn), lambda i, j: (0, j)),
                  pl.BlockSpec((d, GATE_PAD), lambda i, j: (0, 0))],
        out_specs=[pl.BlockSpec((tm, tn), lambda i, j: (i, j)),
                   pl.BlockSpec((tm, GATE_PAD), lambda i, j: (i, 0))],
        out_shape=[jax.ShapeDtypeStruct((m, n), BF16),
                   jax.ShapeDtypeStruct((m, GATE_PAD), F32)],
        scratch_shapes=[pltpu.VMEM((tm, d), BF16), pltpu.VMEM((tm, LANES), F32),
                        pltpu.VMEM((2, SUBLANES, d), F32)],
        compiler_params=_cparams(2),
        name="in_proj",
    )(x2, mod3, mod3, norm_g, w_bf, wg_bf)


def _t5_bucket_table():
    half = N_BUCKETS // 2
    max_exact = half // 2
    qi = np.arange(BLOCK)[:, None]
    kj = np.arange(3 * BLOCK)[None, :]
    rel = kj - BLOCK - qi
    n = np.abs(rel)
    nf = np.maximum(n, 1).astype(np.float32)
    large = max_exact + (np.log(nf / max_exact) / math.log(MAX_DISTANCE / max_exact)
                         * (half - max_exact)).astype(np.int32)
    large = np.minimum(large, half - 1)
    return (np.where(rel > 0, half, 0) + np.where(n < max_exact, n, large)).astype(np.int32)


def _attn_kernel(rb_ref, sink_ref, bucket_ref, q_ref, k_ref, v_ref, o_ref, tbl_ref, *, nb):
    b = pl.program_id(0)
    n = pl.program_id(1)

    @pl.when((b == 0) & (n == 0))
    def _():
        qi = lax.broadcasted_iota(jnp.int32, (BLOCK, 3 * BLOCK), 0)
        kj = lax.broadcasted_iota(jnp.int32, (BLOCK, 3 * BLOCK), 1)
        in_band = jnp.abs(kj - BLOCK - qi) <= WINDOW
        bucket = bucket_ref[...]

        def per_head(h, carry):
            def per_bucket(bk, acc):
                return jnp.where(bucket == bk, rb_ref[bk, h], acc)
            acc = lax.fori_loop(0, N_BUCKETS, per_bucket, jnp.zeros((BLOCK, 3 * BLOCK), F32))
            tbl_ref[h] = jnp.where(in_band, acc, NEG_INF)
            return carry

        lax.fori_loop(0, N_ATTN_HEADS, per_head, 0)

    r_prev = pl.multiple_of(jnp.maximum(n - 1, 0) * BLOCK, BLOCK)
    r_cur = pl.multiple_of(n * BLOCK, BLOCK)
    r_next = pl.multiple_of(jnp.minimum(n + 1, nb - 1) * BLOCK, BLOCK)
    col = lax.broadcasted_iota(jnp.int32, (1, 3 * BLOCK), 1)
    pen = jnp.where((col < BLOCK) & (n == 0), NEG_INF, 0.0)
    pen = jnp.where((col >= 2 * BLOCK) & (n == nb - 1), NEG_INF, pen)
    scale = HEAD_DIM ** -0.5

    for g in range(N_KV_HEADS):
        kc = slice(g * HEAD_DIM, (g + 1) * HEAD_DIM)
        k3 = jnp.concatenate([k_ref[0, pl.ds(r_prev, BLOCK), kc],
                              k_ref[0, pl.ds(r_cur, BLOCK), kc],
                              k_ref[0, pl.ds(r_next, BLOCK), kc]], axis=0)
        v3 = jnp.concatenate([v_ref[0, pl.ds(r_prev, BLOCK), kc],
                              v_ref[0, pl.ds(r_cur, BLOCK), kc],
                              v_ref[0, pl.ds(r_next, BLOCK), kc]], axis=0)
        heads = [g * GQA_GROUP + j for j in range(GQA_GROUP)]
        qs = jnp.concatenate([q_ref[0, :, h * HEAD_DIM:(h + 1) * HEAD_DIM] for h in heads], axis=0)
        s = _dot_nt(qs, k3)
        ps, inv_dens = [], []
        for j, h in enumerate(heads):
            sj = s[j * BLOCK:(j + 1) * BLOCK] * scale + tbl_ref[h] + pen
            sink = sink_ref[0, h]
            m = jnp.maximum(jnp.max(sj, axis=-1, keepdims=True), sink)
            p = jnp.exp(sj - m)
            den = jnp.sum(p, axis=-1, keepdims=True) + jnp.exp(sink - m)
            ps.append(p.astype(BF16))
            inv_dens.append(1.0 / den)
        o = _dot(jnp.concatenate(ps, axis=0), v3)
        for j, h in enumerate(heads):
            o_ref[0, :, h * HEAD_DIM:(h + 1) * HEAD_DIM] = (
                o[j * BLOCK:(j + 1) * BLOCK] * inv_dens[j]).astype(o_ref.dtype)


def _attention(proj3, rel_bias, sink, bucket):
    bsz, seq, _ = proj3.shape
    nb = seq // BLOCK
    smem = pl.BlockSpec(memory_space=pltpu.SMEM)
    return pl.pallas_call(
        functools.partial(_attn_kernel, nb=nb),
        grid=(bsz, nb),
        in_specs=[smem, smem,
                  pl.BlockSpec((BLOCK, 3 * BLOCK), lambda b, n: (0, 0)),
                  pl.BlockSpec((1, BLOCK, ATTN_WIDTH), lambda b, n: (b, n, OFF_QA // ATTN_WIDTH)),
                  pl.BlockSpec((1, seq, KV_WIDTH), lambda b, n: (b, 0, OFF_KA // KV_WIDTH)),
                  pl.BlockSpec((1, seq, KV_WIDTH), lambda b, n: (b, 0, OFF_VA // KV_WIDTH))],
        out_specs=pl.BlockSpec((1, BLOCK, ATTN_WIDTH), lambda b, n: (b, n, 0)),
        out_shape=jax.ShapeDtypeStruct((bsz, seq, ATTN_WIDTH), BF16),
        scratch_shapes=[pltpu.VMEM((N_ATTN_HEADS, BLOCK, 3 * BLOCK), F32)],
        compiler_params=_cparams(2),
        name="window_attn",
    )(rel_bias, sink, bucket, proj3, proj3, proj3)


def _gla_tri_mats():
    t = np.arange(GLA_BLOCK)
    same = (t[:, None] // GLA_CHUNK) == (t[None, :] // GLA_CHUNK)
    fwd = same & (t[None, :] <= t[:, None])
    bwd = same & (t[None, :] >= t[:, None])
    return np.stack([fwd, bwd]).astype(np.float32)


def _gla_kernel(tri_ref, msk_ref, q_ref, k_ref, v_ref, r_ref, gt_ref, wa_ref, ba_ref, ng_ref, o_ref,
                bcum_ref, qt_ref, stb_ref, st_ref, u_ref, acc_ref, *, seq):
    c = GLA_CHUNK
    cpb = GLA_BLOCK // c
    nblk = seq // GLA_BLOCK
    scale = GLA_DK ** -0.5
    anchor = (c // 2 - 1, c // 2)
    last = (c - 1, 0)

    def gates(r0):
        for sub in range(2):
            rows = pl.ds(r0 + sub * GLA_BLOCK, GLA_BLOCK)
            gtb = gt_ref[0, rows, :].astype(BF16)
            for d in range(2):
                z = _dot(gtb, wa_ref[d]) + ba_ref[d]
                lg = (jnp.minimum(z, 0.0) - jnp.log(1.0 + jnp.exp(-jnp.abs(z)))) * (1.0 / GATE_TEMP)
                hi = lg.astype(BF16)
                lo = (lg - hi.astype(F32)).astype(BF16)
                cum = _dot(tri_ref[d], jnp.concatenate([hi, lo], axis=1))
                bcum_ref[d, rows, :] = cum[:, :GLA_DK] + cum[:, GLA_DK:]

    _row_loop(seq, 2 * GLA_BLOCK, gates)

    st_ref[...] = jnp.zeros_like(st_ref)

    def scan_block(i, carry):
        for d, bi in ((0, i), (1, nblk - 1 - i)):
            r0 = pl.multiple_of(bi * GLA_BLOCK, GLA_BLOCK)
            lanes = slice(d * GLA_DK, (d + 1) * GLA_DK)
            decays, qps, kps = [], [], []
            for ci in range(cpb):
                rows = pl.ds(r0 + ci * c, c)
                bc = bcum_ref[d, rows, :]
                b_a = bc[anchor[d]:anchor[d] + 1]
                b_l = bc[last[d]:last[d] + 1]
                qpf = q_ref[0, rows, :].astype(F32) * scale * jnp.exp(bc - b_a)
                kpf = k_ref[0, rows, :].astype(F32) * jnp.exp(b_a - bc)
                qps.append(qpf.astype(BF16))
                kps.append(kpf.astype(BF16))
                qt_ref[rows, lanes] = (qpf * jnp.exp(b_a)).astype(BF16)
                kd = (kpf * jnp.exp(b_l - b_a)).astype(BF16)
                u_ref[d, ci] = _dot_tn(v_ref[0, rows, :], kd)
                decays.append(jnp.exp(b_l))
            rows = pl.ds(r0, GLA_BLOCK)
            s = _dot_nt(jnp.concatenate(qps, axis=0), jnp.concatenate(kps, axis=0))
            a = jnp.where(msk_ref[d] > 0.5, s, 0.0).astype(BF16)
            acc_ref[d, rows, :] = _dot(a, v_ref[0, rows, :])
            for s in range(GLA_DV // GLA_SLAB):
                sl = slice(s * GLA_SLAB, (s + 1) * GLA_SLAB)
                st = st_ref[d, sl, :]
                for ci in (range(cpb) if d == 0 else reversed(range(cpb))):
                    stb_ref[bi * cpb + ci, sl, lanes] = st.astype(BF16)
                    st = st * decays[ci] + u_ref[d, ci, sl, :]
                st_ref[d, sl, :] = st
        return carry

    lax.fori_loop(0, nblk, scan_block, 0)

    ng = ng_ref[...]

    def post(r0):
        for ci in range(cpb):
            rows = pl.ds(r0 + ci * c, c)
            o = (acc_ref[0, rows, :] + acc_ref[1, rows, :]
                 + _dot_nt(qt_ref[rows, :], stb_ref[lax.div(r0, c) + ci]))
            ms = jnp.sum(o * o, axis=-1, keepdims=True) * (1.0 / GLA_DV)
            on = o * lax.rsqrt(ms + EPS) * ng
            o_ref[0, rows, :] = (on * _silu(r_ref[0, rows, :].astype(F32))).astype(o_ref.dtype)

    _row_loop(seq, GLA_BLOCK, post)


def _gla(proj3, gates3, tri, wa_pad, ba, norm_g):
    bsz, seq, _ = proj3.shape
    nc = seq // GLA_CHUNK
    whole = lambda shape: pl.BlockSpec(shape, lambda b, h: (0,) * len(shape))
    return pl.pallas_call(
        functools.partial(_gla_kernel, seq=seq),
        grid=(bsz, N_GLA_HEADS),
        in_specs=[whole((2, GLA_BLOCK, GLA_BLOCK)),
                  whole((2, GLA_BLOCK, GLA_BLOCK)),
                  pl.BlockSpec((1, seq, GLA_DK), lambda b, h: (b, 0, OFF_QG // GLA_DK + h)),
                  pl.BlockSpec((1, seq, GLA_DK), lambda b, h: (b, 0, OFF_KG // GLA_DK + h)),
                  pl.BlockSpec((1, seq, GLA_DV), lambda b, h: (b, 0, OFF_VG // GLA_DV + h)),
                  pl.BlockSpec((1, seq, GLA_DV), lambda b, h: (b, 0, OFF_RG // GLA_DV + h)),
                  pl.BlockSpec((1, seq, GATE_PAD), lambda b, h: (b, 0, 0)),
                  pl.BlockSpec((2, GATE_PAD, GLA_DK), lambda b, h: (0, 0, h)),
                  pl.BlockSpec((2, 1, GLA_DK), lambda b, h: (0, 0, h)),
                  whole((1, GLA_DV))],
        out_specs=pl.BlockSpec((1, seq, GLA_DV), lambda b, h: (b, 0, h)),
        out_shape=jax.ShapeDtypeStruct((bsz, seq, GLA_V_WIDTH), BF16),
        scratch_shapes=[pltpu.VMEM((2, seq, GLA_DK), F32),
                        pltpu.VMEM((seq, 2 * GLA_DK), BF16),
                        pltpu.VMEM((nc, GLA_DV, 2 * GLA_DK), BF16),
                        pltpu.VMEM((2, GLA_DV, GLA_DK), F32),
                        pltpu.VMEM((2, GLA_BLOCK // GLA_CHUNK, GLA_DV, GLA_DK), F32),
                        pltpu.VMEM((2, seq, GLA_DV), F32)],
        compiler_params=_cparams(2),
        name="gla",
    )(tri.astype(BF16), tri, proj3, proj3, proj3, proj3, gates3, wa_pad, ba, norm_g)


def _outproj_kernel(a_ref, g_ref, wa_ref, wg_ref, x_ref, g1_ref, o_ref):
    acc = _dot(a_ref[...], wa_ref[...]) + _dot(g_ref[...], wg_ref[...])
    o_ref[...] = x_ref[...] + g1_ref[0] * acc


def _out_proj(attn2, gla2, w_bf, x2, mod3, b0, seq, tm=512, tn=1024):
    m, d = x2.shape
    ka = attn2.shape[1]
    kg = gla2.shape[1]
    return pl.pallas_call(
        _outproj_kernel,
        grid=(m // tm, d // tn),
        in_specs=[pl.BlockSpec((tm, ka), lambda i, j: (i, 0)),
                  pl.BlockSpec((tm, kg), lambda i, j: (i, 0)),
                  pl.BlockSpec((ka, tn), lambda i, j: (0, j)),
                  pl.BlockSpec((kg, tn), lambda i, j: (ka // kg, j)),
                  pl.BlockSpec((tm, tn), lambda i, j: (i, j)),
                  pl.BlockSpec((1, 1, tn), lambda i, j: ((b0 + (i * tm) // seq) * 6 + 2, 0, j))],
        out_specs=pl.BlockSpec((tm, tn), lambda i, j: (i, j)),
        out_shape=jax.ShapeDtypeStruct((m, d), F32),
        compiler_params=_cparams(2),
        name="out_proj",
    )(attn2, gla2, w_bf, w_bf, x2, mod3)


def _mlp_kernel(x_ref, sh_ref, sc_ref, g2_ref, ng_ref, fg_ref, w1_ref, w2_ref, o_ref, h_ref, r_ref, m_ref,
                *, tn):
    f = pl.program_id(1)
    tm, d = x_ref.shape

    @pl.when(f == 0)
    def _():
        _modulated_rms_rows(x_ref, h_ref, r_ref, m_ref, ng_ref[...], sc_ref[0], sh_ref[0], tm)
        o_ref[...] = jnp.zeros_like(o_ref)

    u = _dot(h_ref[...], w1_ref[...])
    u = jnp.square(jnp.maximum(u, 0.0)).astype(BF16)
    for c in range(d // tn):
        cols = slice(c * tn, (c + 1) * tn)
        o_ref[:, cols] += _dot(u, w2_ref[:, cols])

    @pl.when(f == pl.num_programs(1) - 1)
    def _():
        m_ref[0] = jnp.broadcast_to(fg_ref[...], (SUBLANES, d))
        m_ref[1] = jnp.broadcast_to(g2_ref[0], (SUBLANES, d))

        def residual(rows, cols):
            x2 = x_ref[rows, cols] + m_ref[1, :, cols] * o_ref[rows, cols]
            o_ref[rows, cols] = x2
            return x2

        _row_rsqrt(o_ref, r_ref, tm, load=residual)
        _scale_rows(o_ref, o_ref, r_ref, m_ref, tm, shift=False)


def _mlp(x1, mod3, b0, seq, norm_g, final_g, w1_bf, w2_bf, tm=512, tf=512, tn=512):
    m, d = x1.shape
    dff = w1_bf.shape[1]
    mod_spec = lambda k: pl.BlockSpec((1, 1, d), lambda i, f: ((b0 + (i * tm) // seq) * 6 + k, 0, 0))
    return pl.pallas_call(
        functools.partial(_mlp_kernel, tn=tn),
        grid=(m // tm, dff // tf),
        in_specs=[pl.BlockSpec((tm, d), lambda i, f: (i, 0)),
                  mod_spec(3), mod_spec(4), mod_spec(5),
                  pl.BlockSpec((1, d), lambda i, f: (0, 0)),
                  pl.BlockSpec((1, d), lambda i, f: (0, 0)),
                  pl.BlockSpec((d, tf), lambda i, f: (0, f)),
                  pl.BlockSpec((tf, d), lambda i, f: (f, 0))],
        out_specs=pl.BlockSpec((tm, d), lambda i, f: (i, 0)),
        out_shape=jax.ShapeDtypeStruct((m, d), F32),
        scratch_shapes=[pltpu.VMEM((tm, d), BF16), pltpu.VMEM((tm, LANES), F32),
                        pltpu.VMEM((2, SUBLANES, d), F32)],
        compiler_params=_cparams(2),
        name="mlp",
    )(x1, mod3, mod3, mod3, norm_g, final_g, w1_bf, w2_bf)


def _trunk_group(x, mod3, b0, p):
    bsz, seq, d = x.shape
    x2 = x.reshape(bsz * seq, d)
    proj, gates = _in_proj(x2, mod3, b0, seq, p["norm1_g"], p["w_in"], p["w_gate"])
    proj3 = proj.reshape(bsz, seq, PROJ_WIDTH)
    gates3 = gates.reshape(bsz, seq, GATE_PAD)
    attn = _attention(proj3, p["rel_bias"], p["sink"], p["bucket"])
    gla = _gla(proj3, gates3, p["tri"], p["wa_pad"], p["ba"], p["gla_norm_g"])
    x1 = _out_proj(attn.reshape(bsz * seq, ATTN_WIDTH), gla.reshape(bsz * seq, GLA_V_WIDTH),
                   p["w_out"], x2, mod3, b0, seq)
    y = _mlp(x1, mod3, b0, seq, p["norm2_g"], p["final_g"], p["w_mlp_in"], p["w_mlp_out"])
    return y.reshape(bsz, seq, d)


def kernel(x_prompt, x_sample, c_prompt, c_sample, w_ada, b_ada, norm1_g, w_in, gla_wa_fwd, gla_ba_fwd, gla_wa_bwd, gla_ba_bwd, gla_norm_g, attn_sink, rel_bias, w_out, norm2_g, w_mlp_in, w_mlp_out, final_g):
    assert w_ada.shape[0] == 1, "single-layer trunk"
    d = x_prompt.shape[-1]
    bp, bs = c_prompt.shape[0], c_sample.shape[0]

    rows = -(-(bp + bs) // 16) * 16
    c_pad = jnp.zeros((rows, d), F32).at[:bp].set(c_prompt).at[bp:bp + bs].set(c_sample)
    mod = _ada(c_pad, w_ada[0], b_ada[0])
    mod3 = mod.reshape(rows * 6, 1, d)

    w_in0 = w_in[0]
    wa_pad = jnp.zeros((2, GATE_PAD, GLA_K_WIDTH), F32)
    wa_pad = wa_pad.at[0, :GATE_RANK].set(gla_wa_fwd[0]).at[1, GATE_RANK:2 * GATE_RANK].set(gla_wa_bwd[0])
    p = {
        "norm1_g": norm1_g[0].reshape(1, d),
        "norm2_g": norm2_g[0].reshape(1, d),
        "final_g": final_g.reshape(1, d),
        "w_in": w_in0[:, :PROJ_WIDTH].astype(BF16),
        "w_gate": jnp.pad(w_in0[:, OFF_GATE:], ((0, 0), (0, GATE_PAD - 2 * GATE_RANK))).astype(BF16),
        "w_out": w_out[0].astype(BF16),
        "w_mlp_in": w_mlp_in[0].astype(BF16),
        "w_mlp_out": w_mlp_out[0].astype(BF16),
        "wa_pad": wa_pad.astype(BF16),
        "ba": jnp.stack([gla_ba_fwd[0], gla_ba_bwd[0]]).reshape(2, 1, GLA_K_WIDTH),
        "gla_norm_g": gla_norm_g[0].reshape(1, GLA_DV),
        "rel_bias": rel_bias,
        "sink": attn_sink[0].reshape(1, N_ATTN_HEADS),
        "bucket": jnp.asarray(_t5_bucket_table()),
        "tri": jnp.asarray(_gla_tri_mats()),
    }
    y_prompt = _trunk_group(x_prompt, mod3, 0, p)
    y_sample = _trunk_group(x_sample, mod3, bp, p)
    return (y_prompt, y_sample)
```

```python
import functools
import math

import numpy as np
import jax
import jax.numpy as jnp
from jax import lax
from jax.experimental import pallas as pl
from jax.experimental.pallas import tpu as pltpu

F32 = jnp.float32
BF16 = jnp.bfloat16

HEAD_DIM = 128
N_ATTN_HEADS = 16
N_KV_HEADS = 4
GQA_GROUP = N_ATTN_HEADS // N_KV_HEADS
ATTN_WIDTH = N_ATTN_HEADS * HEAD_DIM
KV_WIDTH = N_KV_HEADS * HEAD_DIM
WINDOW = 128
BLOCK = 128
GLA_DV = 256
GLA_DK = 128
N_GLA_HEADS = 8
GLA_K_WIDTH = N_GLA_HEADS * GLA_DK
GLA_V_WIDTH = N_GLA_HEADS * GLA_DV
GATE_RANK = 16
GATE_TEMP = 16.0
N_BUCKETS = 32
MAX_DISTANCE = 128
EPS = 1e-6
NEG_INF = -1e30
LOG2E = math.log2(math.e)

OFF_QA = 0
OFF_KA = OFF_QA + ATTN_WIDTH
OFF_VA = OFF_KA + KV_WIDTH
OFF_QG = OFF_VA + KV_WIDTH
OFF_KG = OFF_QG + GLA_K_WIDTH
OFF_VG = OFF_KG + GLA_K_WIDTH
OFF_RG = OFF_VG + GLA_V_WIDTH
OFF_GATE = OFF_RG + GLA_V_WIDTH
PROJ_WIDTH = OFF_GATE
GATE_PAD = 128

GLA_CHUNK = 64
GLA_BLOCK = 256
GLA_SLAB = 64
SUBLANES = 8
LANES = 128
NORM_ROWS = 16
RSQRT_GROUPS = 8

VMEM_LIMIT = 60 * 1024 * 1024


def _cparams(n_axes, vmem=VMEM_LIMIT):
    return pltpu.CompilerParams(dimension_semantics=("arbitrary",) * n_axes,
                                vmem_limit_bytes=vmem)


def _dot(a, b):
    return jnp.dot(a, b, preferred_element_type=F32)


def _dot_nt(a, b):
    return lax.dot_general(a, b, (((1,), (1,)), ((), ())), preferred_element_type=F32)


def _dot_tn(a, b):
    return lax.dot_general(a, b, (((0,), (0,)), ((), ())), preferred_element_type=F32)


def _silu(x):
    return x * (1.0 / (1.0 + jnp.exp(-x)))


def _row_loop(n_rows, rows_per, body):
    def step(i, carry):
        body(pl.multiple_of(i * rows_per, rows_per))
        return carry
    lax.fori_loop(0, n_rows // rows_per, step, 0)


def _ada_kernel(c_ref, w_ref, b_ref, o_ref):
    a = _silu(c_ref[...]).astype(BF16)
    o_ref[...] = _dot(a, w_ref[...].astype(BF16)) + b_ref[...]


def _ada(c_pad, w_ada, b_ada, tn=512):
    rows, d = c_pad.shape
    n = w_ada.shape[1]
    return pl.pallas_call(
        _ada_kernel,
        grid=(n // tn,),
        in_specs=[pl.BlockSpec((rows, d), lambda j: (0, 0)),
                  pl.BlockSpec((d, tn), lambda j: (0, j)),
                  pl.BlockSpec((1, tn), lambda j: (0, j))],
        out_specs=pl.BlockSpec((rows, tn), lambda j: (0, j)),
        out_shape=jax.ShapeDtypeStruct((rows, n), F32),
        compiler_params=_cparams(1),
        name="ada",
    )(c_pad, w_ada, b_ada.reshape(1, n))


def _row_rsqrt(x_ref, r_ref, n_rows, load=None):
    d = x_ref.shape[-1]

    def body(r0):
        for sub in range(RSQRT_GROUPS):
            rows = pl.ds(r0 + sub * SUBLANES, SUBLANES)
            acc = jnp.zeros((SUBLANES, LANES), F32)
            for j in range(d // LANES):
                cols = slice(j * LANES, (j + 1) * LANES)
                x = x_ref[rows, cols] if load is None else load(rows, cols)
                acc = acc + x * x
            ms = jnp.sum(acc, axis=-1, keepdims=True) * (1.0 / d)
            r_ref[rows, :] = jnp.broadcast_to(lax.rsqrt(ms + EPS), (SUBLANES, LANES))

    _row_loop(n_rows, RSQRT_GROUPS * SUBLANES, body)


def _scale_rows(x_ref, o_ref, r_ref, m_ref, n_rows, shift):
    d = x_ref.shape[-1]
    reps = NORM_ROWS // SUBLANES

    def body(r0):
        rows = pl.ds(r0, NORM_ROWS)
        r = r_ref[rows, :]
        for j in range(d // LANES):
            cols = slice(j * LANES, (j + 1) * LANES)
            y = x_ref[rows, cols] * r * jnp.concatenate([m_ref[0, :, cols]] * reps, axis=0)
            if shift:
                y = y + jnp.concatenate([m_ref[1, :, cols]] * reps, axis=0)
            o_ref[rows, cols] = y.astype(o_ref.dtype)

    _row_loop(n_rows, NORM_ROWS, body)


def _modulated_rms_rows(x_ref, h_ref, r_ref, m_ref, g, sc, sh, n_rows):
    d = x_ref.shape[-1]
    m_ref[0] = jnp.broadcast_to(g * (1.0 + sc), (SUBLANES, d))
    m_ref[1] = jnp.broadcast_to(sh, (SUBLANES, d))
    _row_rsqrt(x_ref, r_ref, n_rows)
    _scale_rows(x_ref, h_ref, r_ref, m_ref, n_rows, shift=True)


def _inproj_kernel(x_ref, sh_ref, sc_ref, g_ref, w_ref, wg_ref, o_ref, og_ref, h_ref, r_ref, m_ref):
    @pl.when(pl.program_id(1) == 0)
    def _():
        _modulated_rms_rows(x_ref, h_ref, r_ref, m_ref, g_ref[...], sc_ref[0], sh_ref[0], x_ref.shape[0])
        og_ref[...] = _dot(h_ref[...], wg_ref[...])

    o_ref[...] = _dot(h_ref[...], w_ref[...]).astype(o_ref.dtype)


def _in_proj(x2, mod3, b0, seq, norm_g, w_bf, wg_bf, tm=512, tn=1024):
    m, d = x2.shape
    n = min(w_bf.shape[1], PROJ_WIDTH)
    assert n % tn == 0
    mod_spec = lambda k: pl.BlockSpec((1, 1, d), lambda i, j: ((b0 + (i * tm) // seq) * 6 + k, 0, 0))
    return pl.pallas_call(
        _inproj_kernel,
        grid=(m // tm, n // tn),
        in_specs=[pl.BlockSpec((tm, d), lambda i, j: (i, 0)),
                  mod_spec(0), mod_spec(1),
                  pl.BlockSpec((1, d), lambda i, j: (0, 0)),
                  pl.BlockSpec((d, tn), lambda i, j: (0, j)),
                  pl.BlockSpec((d, GATE_PAD), lambda i, j: (0, 0))],
        out_specs=[pl.BlockSpec((tm, tn), lambda i, j: (i, j)),
                   pl.BlockSpec((tm, GATE_PAD), lambda i, j: (i, 0))],
        out_shape=[jax.ShapeDtypeStruct((m, n), BF16),
                   jax.ShapeDtypeStruct((m, GATE_PAD), F32)],
        scratch_shapes=[pltpu.VMEM((tm, d), BF16), pltpu.VMEM((tm, LANES), F32),
                        pltpu.VMEM((2, SUBLANES, d), F32)],
        compiler_params=_cparams(2),
        name="in_proj",
    )(x2, mod3, mod3, norm_g, w_bf, wg_bf)


def _t5_bucket_table():
    half = N_BUCKETS // 2
    max_exact = half // 2
    qi = np.arange(BLOCK)[:, None]
    kj = np.arange(3 * BLOCK)[None, :]
    rel = kj - BLOCK - qi
    n = np.abs(rel)
    nf = np.maximum(n, 1).astype(np.float32)
    large = max_exact + (np.log(nf / max_exact) / math.log(MAX_DISTANCE / max_exact)
                         * (half - max_exact)).astype(np.int32)
    large = np.minimum(large, half - 1)
    return (np.where(rel > 0, half, 0) + np.where(n < max_exact, n, large)).astype(np.int32)


def _attn_kernel(rb_ref, sink_ref, bucket_ref, q_ref, k_ref, v_ref, o_ref, tbl_ref, *, nb):
    b = pl.program_id(0)
    n = pl.program_id(1)

    @pl.when((b == 0) & (n == 0))
    def _():
        qi = lax.broadcasted_iota(jnp.int32, (BLOCK, 3 * BLOCK), 0)
        kj = lax.broadcasted_iota(jnp.int32, (BLOCK, 3 * BLOCK), 1)
        in_band = jnp.abs(kj - BLOCK - qi) <= WINDOW
        bucket = bucket_ref[...]

        def per_head(h, carry):
            def per_bucket(bk, acc):
                return jnp.where(bucket == bk, rb_ref[bk, h], acc)
            acc = lax.fori_loop(0, N_BUCKETS, per_bucket, jnp.zeros((BLOCK, 3 * BLOCK), F32)) * LOG2E
            for v in range(4):
                keep = in_band
                if v & 1:
                    keep = keep & (kj >= BLOCK)
                if v & 2:
                    keep = keep & (kj < 2 * BLOCK)
                tbl_ref[v, h] = jnp.where(keep, acc, NEG_INF)
            return carry

        lax.fori_loop(0, N_ATTN_HEADS, per_head, 0)

    r_prev = pl.multiple_of(jnp.maximum(n - 1, 0) * BLOCK, BLOCK)
    r_cur = pl.multiple_of(n * BLOCK, BLOCK)
    r_next = pl.multiple_of(jnp.minimum(n + 1, nb - 1) * BLOCK, BLOCK)
    variant = (n == 0).astype(jnp.int32) + 2 * (n == nb - 1).astype(jnp.int32)
    scale2 = HEAD_DIM ** -0.5 * LOG2E

    for g in range(N_KV_HEADS):
        kc = slice(g * HEAD_DIM, (g + 1) * HEAD_DIM)
        k3 = jnp.concatenate([k_ref[0, pl.ds(r_prev, BLOCK), kc],
                              k_ref[0, pl.ds(r_cur, BLOCK), kc],
                              k_ref[0, pl.ds(r_next, BLOCK), kc]], axis=0)
        v3 = jnp.concatenate([v_ref[0, pl.ds(r_prev, BLOCK), kc],
                              v_ref[0, pl.ds(r_cur, BLOCK), kc],
                              v_ref[0, pl.ds(r_next, BLOCK), kc]], axis=0)
        heads = [g * GQA_GROUP + j for j in range(GQA_GROUP)]
        qs = jnp.concatenate([q_ref[0, :, h * HEAD_DIM:(h + 1) * HEAD_DIM] for h in heads], axis=0)
        s = _dot_nt(qs, k3)
        ps, inv_dens = [], []
        for j, h in enumerate(heads):
            sj = s[j * BLOCK:(j + 1) * BLOCK] * scale2 + tbl_ref[variant, h]
            sink = sink_ref[0, h] * LOG2E
            m = jnp.maximum(jnp.max(sj, axis=-1, keepdims=True), sink)
            p = jnp.exp2(sj - m)
            den = jnp.sum(p, axis=-1, keepdims=True) + jnp.exp2(sink - m)
            ps.append(p.astype(BF16))
            inv_dens.append(1.0 / den)
        o = _dot(jnp.concatenate(ps, axis=0), v3)
        for j, h in enumerate(heads):
            o_ref[0, :, h * HEAD_DIM:(h + 1) * HEAD_DIM] = (
                o[j * BLOCK:(j + 1) * BLOCK] * inv_dens[j]).astype(o_ref.dtype)


def _attention(proj3, rel_bias, sink, bucket):
    bsz, seq, _ = proj3.shape
    nb = seq // BLOCK
    smem = pl.BlockSpec(memory_space=pltpu.SMEM)
    return pl.pallas_call(
        functools.partial(_attn_kernel, nb=nb),
        grid=(bsz, nb),
        in_specs=[smem, smem,
                  pl.BlockSpec((BLOCK, 3 * BLOCK), lambda b, n: (0, 0)),
                  pl.BlockSpec((1, BLOCK, ATTN_WIDTH), lambda b, n: (b, n, OFF_QA // ATTN_WIDTH)),
                  pl.BlockSpec((1, seq, KV_WIDTH), lambda b, n: (b, 0, OFF_KA // KV_WIDTH)),
                  pl.BlockSpec((1, seq, KV_WIDTH), lambda b, n: (b, 0, OFF_VA // KV_WIDTH))],
        out_specs=pl.BlockSpec((1, BLOCK, ATTN_WIDTH), lambda b, n: (b, n, 0)),
        out_shape=jax.ShapeDtypeStruct((bsz, seq, ATTN_WIDTH), BF16),
        scratch_shapes=[pltpu.VMEM((4, N_ATTN_HEADS, BLOCK, 3 * BLOCK), F32)],
        compiler_params=_cparams(2),
        name="window_attn",
    )(rel_bias, sink, bucket, proj3, proj3, proj3)


def _gla_tri_mats():
    t = np.arange(GLA_BLOCK)
    same = (t[:, None] // GLA_CHUNK) == (t[None, :] // GLA_CHUNK)
    fwd = same & (t[None, :] <= t[:, None])
    bwd = same & (t[None, :] >= t[:, None])
    return np.stack([fwd, bwd]).astype(np.float32)


def _gla_kernel(tri_ref, msk_ref, q_ref, k_ref, v_ref, r_ref, gt_ref, wa_ref, ba_ref, ng_ref, o_ref,
                bcum_ref, qt_ref, stb_ref, st_ref, u_ref, acc_ref, *, seq):
    c = GLA_CHUNK
    cpb = GLA_BLOCK // c
    nblk = seq // GLA_BLOCK
    scale = GLA_DK ** -0.5
    anchor = (c // 2 - 1, c // 2)
    last = (c - 1, 0)

    def gates(d, bi):
        rows = pl.ds(pl.multiple_of(bi * GLA_BLOCK, GLA_BLOCK), GLA_BLOCK)
        z = _dot(gt_ref[0, rows, :].astype(BF16), wa_ref[d]) + ba_ref[d]
        lg = (jnp.minimum(z, 0.0) - jnp.log(1.0 + jnp.exp(-jnp.abs(z)))) * (1.0 / GATE_TEMP)
        hi = lg.astype(BF16)
        lo = (lg - hi.astype(F32)).astype(BF16)
        cum = _dot(tri_ref[d], jnp.concatenate([hi, lo], axis=1))
        bcum_ref[d, rows, :] = cum[:, :GLA_DK] + cum[:, GLA_DK:]

    st_ref[...] = jnp.zeros_like(st_ref)
    gates(0, 0)
    gates(1, nblk - 1)

    def scan_block(i, carry):
        for d, bi in ((0, i), (1, nblk - 1 - i)):
            r0 = pl.multiple_of(bi * GLA_BLOCK, GLA_BLOCK)
            lanes = slice(d * GLA_DK, (d + 1) * GLA_DK)
            decays, qps, kps = [], [], []
            for ci in range(cpb):
                rows = pl.ds(r0 + ci * c, c)
                bc = bcum_ref[d, rows, :]
                b_a = bc[anchor[d]:anchor[d] + 1]
                b_l = bc[last[d]:last[d] + 1]
                qpf = q_ref[0, rows, :].astype(F32) * scale * jnp.exp(bc - b_a)
                kpf = k_ref[0, rows, :].astype(F32) * jnp.exp(b_a - bc)
                qps.append(qpf.astype(BF16))
                kps.append(kpf.astype(BF16))
                qt_ref[rows, lanes] = (qpf * jnp.exp(b_a)).astype(BF16)
                kd = (kpf * jnp.exp(b_l - b_a)).astype(BF16)
                u_ref[d, ci] = _dot_tn(v_ref[0, rows, :], kd)
                decays.append(jnp.exp(b_l))
            rows = pl.ds(r0, GLA_BLOCK)
            s = _dot_nt(jnp.concatenate(qps, axis=0), jnp.concatenate(kps, axis=0))
            a = jnp.where(msk_ref[d] > 0.5, s, 0.0).astype(BF16)
            acc_ref[d, rows, :] = _dot(a, v_ref[0, rows, :])
            for s in range(GLA_DV // GLA_SLAB):
                sl = slice(s * GLA_SLAB, (s + 1) * GLA_SLAB)
                st = st_ref[d, sl, :]
                for ci in (range(cpb) if d == 0 else reversed(range(cpb))):
                    stb_ref[bi * cpb + ci, sl, lanes] = st.astype(BF16)
                    st = st * decays[ci] + u_ref[d, ci, sl, :]
                st_ref[d, sl, :] = st
        gates(0, jnp.minimum(i + 1, nblk - 1))
        gates(1, jnp.maximum(nblk - 2 - i, 0))
        return carry

    lax.fori_loop(0, nblk, scan_block, 0)

    ng = ng_ref[...]

    def post(r0):
        for ci in range(2 * cpb):
            rows = pl.ds(r0 + ci * c, c)
            o = (acc_ref[0, rows, :] + acc_ref[1, rows, :]
                 + _dot_nt(qt_ref[rows, :], stb_ref[lax.div(r0, c) + ci]))
            ms = jnp.sum(o * o, axis=-1, keepdims=True) * (1.0 / GLA_DV)
            on = o * lax.rsqrt(ms + EPS) * ng
            o_ref[0, rows, :] = (on * _silu(r_ref[0, rows, :].astype(F32))).astype(o_ref.dtype)

    _row_loop(seq, 2 * GLA_BLOCK, post)


def _gla(proj3, gates3, tri, wa_pad, ba, norm_g):
    bsz, seq, _ = proj3.shape
    nc = seq // GLA_CHUNK
    whole = lambda shape: pl.BlockSpec(shape, lambda b, h: (0,) * len(shape))
    return pl.pallas_call(
        functools.partial(_gla_kernel, seq=seq),
        grid=(bsz, N_GLA_HEADS),
        in_specs=[whole((2, GLA_BLOCK, GLA_BLOCK)),
                  whole((2, GLA_BLOCK, GLA_BLOCK)),
                  pl.BlockSpec((1, seq, GLA_DK), lambda b, h: (b, 0, OFF_QG // GLA_DK + h)),
                  pl.BlockSpec((1, seq, GLA_DK), lambda b, h: (b, 0, OFF_KG // GLA_DK + h)),
                  pl.BlockSpec((1, seq, GLA_DV), lambda b, h: (b, 0, OFF_VG // GLA_DV + h)),
                  pl.BlockSpec((1, seq, GLA_DV), lambda b, h: (b, 0, OFF_RG // GLA_DV + h)),
                  pl.BlockSpec((1, seq, GATE_PAD), lambda b, h: (b, 0, 0)),
                  pl.BlockSpec((2, GATE_PAD, GLA_DK), lambda b, h: (0, 0, h)),
                  pl.BlockSpec((2, 1, GLA_DK), lambda b, h: (0, 0, h)),
                  whole((1, GLA_DV))],
        out_specs=pl.BlockSpec((1, seq, GLA_DV), lambda b, h: (b, 0, h)),
        out_shape=jax.ShapeDtypeStruct((bsz, seq, GLA_V_WIDTH), BF16),
        scratch_shapes=[pltpu.VMEM((2, seq, GLA_DK), F32),
                        pltpu.VMEM((seq, 2 * GLA_DK), BF16),
                        pltpu.VMEM((nc, GLA_DV, 2 * GLA_DK), BF16),
                        pltpu.VMEM((2, GLA_DV, GLA_DK), F32),
                        pltpu.VMEM((2, GLA_BLOCK // GLA_CHUNK, GLA_DV, GLA_DK), F32),
                        pltpu.VMEM((2, seq, GLA_DV), F32)],
        compiler_params=_cparams(2),
        name="gla",
    )(tri.astype(BF16), tri, proj3, proj3, proj3, proj3, gates3, wa_pad, ba, norm_g)


def _outproj_kernel(a_ref, g_ref, wa_ref, wg_ref, x_ref, g1_ref, o_ref):
    acc = _dot(a_ref[...], wa_ref[...]) + _dot(g_ref[...], wg_ref[...])
    o_ref[...] = x_ref[...] + g1_ref[0] * acc


def _out_proj(attn2, gla2, w_bf, x2, mod3, b0, seq, tm=512, tn=1024):
    m, d = x2.shape
    ka = attn2.shape[1]
    kg = gla2.shape[1]
    return pl.pallas_call(
        _outproj_kernel,
        grid=(m // tm, d // tn),
        in_specs=[pl.BlockSpec((tm, ka), lambda i, j: (i, 0)),
                  pl.BlockSpec((tm, kg), lambda i, j: (i, 0)),
                  pl.BlockSpec((ka, tn), lambda i, j: (0, j)),
                  pl.BlockSpec((kg, tn), lambda i, j: (ka // kg, j)),
                  pl.BlockSpec((tm, tn), lambda i, j: (i, j)),
                  pl.BlockSpec((1, 1, tn), lambda i, j: ((b0 + (i * tm) // seq) * 6 + 2, 0, j))],
        out_specs=pl.BlockSpec((tm, tn), lambda i, j: (i, j)),
        out_shape=jax.ShapeDtypeStruct((m, d), F32),
        compiler_params=_cparams(2),
        name="out_proj",
    )(attn2, gla2, w_bf, w_bf, x2, mod3)


def _mlp_kernel(x_ref, sh_ref, sc_ref, g2_ref, ng_ref, fg_ref, w1_hbm, w2_hbm, o_ref,
                h_ref, r_ref, m_ref, w1_buf, w2_buf, sem, *, tf, tn):
    i = pl.program_id(0)
    tm, d = x_ref.shape
    nf = w1_hbm.shape[1] // tf

    def weight_copies(f, slot):
        return (pltpu.make_async_copy(w1_hbm.at[:, pl.ds(f * tf, tf)], w1_buf.at[slot], sem.at[0, slot]),
                pltpu.make_async_copy(w2_hbm.at[pl.ds(f * tf, tf), :], w2_buf.at[slot], sem.at[1, slot]))

    @pl.when(i == 0)
    def _():
        for cp in weight_copies(0, 0):
            cp.start()

    _modulated_rms_rows(x_ref, h_ref, r_ref, m_ref, ng_ref[...], sc_ref[0], sh_ref[0], tm)
    o_ref[...] = jnp.zeros_like(o_ref)

    def tile_pair(k, carry):
        for slot in range(2):
            f = 2 * k + slot
            for cp in weight_copies(f, slot):
                cp.wait()
            f_next = jnp.where(f + 1 == nf, 0, f + 1)
            for cp in weight_copies(f_next, 1 - slot):
                cp.start()
            u = _dot(h_ref[...], w1_buf[slot])
            u = jnp.square(jnp.maximum(u, 0.0)).astype(BF16)
            for c in range(d // tn):
                cols = slice(c * tn, (c + 1) * tn)
                o_ref[:, cols] += _dot(u, w2_buf[slot, :, cols])
        return carry

    lax.fori_loop(0, nf // 2, tile_pair, 0)

    @pl.when(i == pl.num_programs(0) - 1)
    def _():
        for cp in weight_copies(0, 0):
            cp.wait()

    m_ref[0] = jnp.broadcast_to(fg_ref[...], (SUBLANES, d))
    m_ref[1] = jnp.broadcast_to(g2_ref[0], (SUBLANES, d))

    def residual(rows, cols):
        x2 = x_ref[rows, cols] + m_ref[1, :, cols] * o_ref[rows, cols]
        o_ref[rows, cols] = x2
        return x2

    _row_rsqrt(o_ref, r_ref, tm, load=residual)
    _scale_rows(o_ref, o_ref, r_ref, m_ref, tm, shift=False)


def _mlp(x1, mod3, b0, seq, norm_g, final_g, w1_bf, w2_bf, tm=512, tf=512, tn=512):
    m, d = x1.shape
    dff = w1_bf.shape[1]
    assert (dff // tf) % 2 == 0, "weight tiles are consumed in slot pairs"
    mod_spec = lambda k: pl.BlockSpec((1, 1, d), lambda i: ((b0 + (i * tm) // seq) * 6 + k, 0, 0))
    return pl.pallas_call(
        functools.partial(_mlp_kernel, tf=tf, tn=tn),
        grid=(m // tm,),
        in_specs=[pl.BlockSpec((tm, d), lambda i: (i, 0)),
                  mod_spec(3), mod_spec(4), mod_spec(5),
                  pl.BlockSpec((1, d), lambda i: (0, 0)),
                  pl.BlockSpec((1, d), lambda i: (0, 0)),
                  pl.BlockSpec(memory_space=pl.ANY),
                  pl.BlockSpec(memory_space=pl.ANY)],
        out_specs=pl.BlockSpec((tm, d), lambda i: (i, 0)),
        out_shape=jax.ShapeDtypeStruct((m, d), F32),
        scratch_shapes=[pltpu.VMEM((tm, d), BF16), pltpu.VMEM((tm, LANES), F32),
                        pltpu.VMEM((2, SUBLANES, d), F32),
                        pltpu.VMEM((2, d, tf), BF16), pltpu.VMEM((2, tf, d), BF16),
                        pltpu.SemaphoreType.DMA((2, 2))],
        compiler_params=_cparams(1),
        name="mlp",
    )(x1, mod3, mod3, mod3, norm_g, final_g, w1_bf, w2_bf)


def _trunk_group(x, mod3, b0, p):
    bsz, seq, d = x.shape
    x2 = x.reshape(bsz * seq, d)
    proj, gates = _in_proj(x2, mod3, b0, seq, p["norm1_g"], p["w_in"], p["w_gate"])
    proj3 = proj.reshape(bsz, seq, PROJ_WIDTH)
    gates3 = gates.reshape(bsz, seq, GATE_PAD)
    attn = _attention(proj3, p["rel_bias"], p["sink"], p["bucket"])
    gla = _gla(proj3, gates3, p["tri"], p["wa_pad"], p["ba"], p["gla_norm_g"])
    x1 = _out_proj(attn.reshape(bsz * seq, ATTN_WIDTH), gla.reshape(bsz * seq, GLA_V_WIDTH),
                   p["w_out"], x2, mod3, b0, seq)
    y = _mlp(x1, mod3, b0, seq, p["norm2_g"], p["final_g"], p["w_mlp_in"], p["w_mlp_out"])
    return y.reshape(bsz, seq, d)


def kernel(x_prompt, x_sample, c_prompt, c_sample, w_ada, b_ada, norm1_g, w_in, gla_wa_fwd, gla_ba_fwd, gla_wa_bwd, gla_ba_bwd, gla_norm_g, attn_sink, rel_bias, w_out, norm2_g, w_mlp_in, w_mlp_out, final_g):
    assert w_ada.shape[0] == 1, "single-layer trunk"
    d = x_prompt.shape[-1]
    bp, bs = c_prompt.shape[0], c_sample.shape[0]

    rows = -(-(bp + bs) // 16) * 16
    c_pad = jnp.zeros((rows, d), F32).at[:bp].set(c_prompt).at[bp:bp + bs].set(c_sample)
    mod = _ada(c_pad, w_ada[0], b_ada[0])
    mod3 = mod.reshape(rows * 6, 1, d)

    w_in0 = w_in[0]
    wa_pad = jnp.zeros((2, GATE_PAD, GLA_K_WIDTH), F32)
    wa_pad = wa_pad.at[0, :GATE_RANK].set(gla_wa_fwd[0]).at[1, GATE_RANK:2 * GATE_RANK].set(gla_wa_bwd[0])
    p = {
        "norm1_g": norm1_g[0].reshape(1, d),
        "norm2_g": norm2_g[0].reshape(1, d),
        "final_g": final_g.reshape(1, d),
        "w_in": w_in0.astype(BF16),
        "w_gate": jnp.pad(w_in0[:, OFF_GATE:], ((0, 0), (0, GATE_PAD - 2 * GATE_RANK))).astype(BF16),
        "w_out": w_out[0].astype(BF16),
        "w_mlp_in": w_mlp_in[0].astype(BF16),
        "w_mlp_out": w_mlp_out[0].astype(BF16),
        "wa_pad": wa_pad.astype(BF16),
        "ba": jnp.stack([gla_ba_fwd[0], gla_ba_bwd[0]]).reshape(2, 1, GLA_K_WIDTH),
        "gla_norm_g": gla_norm_g[0].reshape(1, GLA_DV),
        "rel_bias": rel_bias,
        "sink": attn_sink[0].reshape(1, N_ATTN_HEADS),
        "bucket": jnp.asarray(_t5_bucket_table()),
        "tri": jnp.asarray(_gla_tri_mats()),
    }
    y_prompt = _trunk_group(x_prompt, mod3, 0, p)
    y_sample = _trunk_group(x_sample, mod3, bp, p)
    return (y_prompt, y_sample)
```

```python
import functools
import math

import numpy as np
import jax
import jax.numpy as jnp
from jax import lax
from jax.experimental import pallas as pl
from jax.experimental.pallas import tpu as pltpu

F32 = jnp.float32
BF16 = jnp.bfloat16

HEAD_DIM = 128
N_ATTN_HEADS = 16
N_KV_HEADS = 4
GQA_GROUP = N_ATTN_HEADS // N_KV_HEADS
ATTN_WIDTH = N_ATTN_HEADS * HEAD_DIM
KV_WIDTH = N_KV_HEADS * HEAD_DIM
WINDOW = 128
BLOCK = 128
GLA_DV = 256
GLA_DK = 128
N_GLA_HEADS = 8
GLA_K_WIDTH = N_GLA_HEADS * GLA_DK
GLA_V_WIDTH = N_GLA_HEADS * GLA_DV
GATE_RANK = 16
GATE_TEMP = 16.0
N_BUCKETS = 32
MAX_DISTANCE = 128
EPS = 1e-6
NEG_INF = -1e30
LOG2E = math.log2(math.e)

OFF_QA = 0
OFF_KA = OFF_QA + ATTN_WIDTH
OFF_VA = OFF_KA + KV_WIDTH
OFF_QG = OFF_VA + KV_WIDTH
OFF_KG = OFF_QG + GLA_K_WIDTH
OFF_VG = OFF_KG + GLA_K_WIDTH
OFF_RG = OFF_VG + GLA_V_WIDTH
OFF_GATE = OFF_RG + GLA_V_WIDTH
PROJ_WIDTH = OFF_GATE
GATE_PAD = 128

GLA_CHUNK = 64
GLA_BLOCK = 256
GLA_SLAB = 64
GLA_UNROLL = 2
SUBLANES = 8
LANES = 128
NORM_ROWS = 16
RSQRT_GROUPS = 8

VMEM_LIMIT = 60 * 1024 * 1024


def _cparams(n_axes, vmem=VMEM_LIMIT):
    return pltpu.CompilerParams(dimension_semantics=("arbitrary",) * n_axes,
                                vmem_limit_bytes=vmem)


def _dot(a, b):
    return jnp.dot(a, b, preferred_element_type=F32)


def _dot_nt(a, b):
    return lax.dot_general(a, b, (((1,), (1,)), ((), ())), preferred_element_type=F32)


def _dot_tn(a, b):
    return lax.dot_general(a, b, (((0,), (0,)), ((), ())), preferred_element_type=F32)


def _silu(x):
    return x * (1.0 / (1.0 + jnp.exp(-x)))


def _row_loop(n_rows, rows_per, body):
    def step(i, carry):
        body(pl.multiple_of(i * rows_per, rows_per))
        return carry
    lax.fori_loop(0, n_rows // rows_per, step, 0)


def _ada_kernel(c_ref, w_ref, b_ref, o_ref):
    a = _silu(c_ref[...]).astype(BF16)
    o_ref[...] = _dot(a, w_ref[...].astype(BF16)) + b_ref[...]


def _ada(c_pad, w_ada, b_ada, tn=512):
    rows, d = c_pad.shape
    n = w_ada.shape[1]
    return pl.pallas_call(
        _ada_kernel,
        grid=(n // tn,),
        in_specs=[pl.BlockSpec((rows, d), lambda j: (0, 0)),
                  pl.BlockSpec((d, tn), lambda j: (0, j)),
                  pl.BlockSpec((1, tn), lambda j: (0, j))],
        out_specs=pl.BlockSpec((rows, tn), lambda j: (0, j)),
        out_shape=jax.ShapeDtypeStruct((rows, n), F32),
        compiler_params=_cparams(1),
        name="ada",
    )(c_pad, w_ada, b_ada.reshape(1, n))


def _row_rsqrt(x_ref, r_ref, n_rows, load=None):
    d = x_ref.shape[-1]

    def body(r0):
        for sub in range(RSQRT_GROUPS):
            rows = pl.ds(r0 + sub * SUBLANES, SUBLANES)
            acc = jnp.zeros((SUBLANES, LANES), F32)
            for j in range(d // LANES):
                cols = slice(j * LANES, (j + 1) * LANES)
                x = x_ref[rows, cols] if load is None else load(rows, cols)
                acc = acc + x * x
            ms = jnp.sum(acc, axis=-1, keepdims=True) * (1.0 / d)
            r_ref[rows, :] = jnp.broadcast_to(lax.rsqrt(ms + EPS), (SUBLANES, LANES))

    _row_loop(n_rows, RSQRT_GROUPS * SUBLANES, body)


def _scale_rows(x_ref, o_ref, r_ref, m_ref, n_rows, shift):
    d = x_ref.shape[-1]
    reps = NORM_ROWS // SUBLANES

    def body(r0):
        rows = pl.ds(r0, NORM_ROWS)
        r = r_ref[rows, :]
        for j in range(d // LANES):
            cols = slice(j * LANES, (j + 1) * LANES)
            y = x_ref[rows, cols] * r * jnp.concatenate([m_ref[0, :, cols]] * reps, axis=0)
            if shift:
                y = y + jnp.concatenate([m_ref[1, :, cols]] * reps, axis=0)
            o_ref[rows, cols] = y.astype(o_ref.dtype)

    _row_loop(n_rows, NORM_ROWS, body)


def _modulated_rms_rows(x_ref, h_ref, r_ref, m_ref, g, sc, sh, n_rows):
    d = x_ref.shape[-1]
    m_ref[0] = jnp.broadcast_to(g * (1.0 + sc), (SUBLANES, d))
    m_ref[1] = jnp.broadcast_to(sh, (SUBLANES, d))
    _row_rsqrt(x_ref, r_ref, n_rows)
    _scale_rows(x_ref, h_ref, r_ref, m_ref, n_rows, shift=True)


def _stream_weight_tiles(n_tiles, copies, consume):
    assert n_tiles % 2 == 0, "tiles are consumed in slot pairs"

    @pl.when(pl.program_id(0) == 0)
    def _():
        for cp in copies(0, 0):
            cp.start()

    def pair(k, carry):
        for slot in range(2):
            t = 2 * k + slot
            for cp in copies(t, slot):
                cp.wait()
            t_next = jnp.where(t + 1 == n_tiles, 0, t + 1)
            for cp in copies(t_next, 1 - slot):
                cp.start()
            consume(t, slot)
        return carry

    lax.fori_loop(0, n_tiles // 2, pair, 0)

    @pl.when(pl.program_id(0) == pl.num_programs(0) - 1)
    def _():
        for cp in copies(0, 0):
            cp.wait()


def _inproj_kernel(x_ref, sh_ref, sc_ref, g_ref, w_hbm, wg_ref, o_ref, og_ref,
                   h_ref, r_ref, m_ref, w_buf, sem, *, tn):
    _modulated_rms_rows(x_ref, h_ref, r_ref, m_ref, g_ref[...], sc_ref[0], sh_ref[0], x_ref.shape[0])
    og_ref[...] = _dot(h_ref[...], wg_ref[...])

    def copies(t, slot):
        return (pltpu.make_async_copy(w_hbm.at[:, pl.ds(t * tn, tn)], w_buf.at[slot], sem.at[slot]),)

    def consume(t, slot):
        cols = pl.ds(pl.multiple_of(t * tn, tn), tn)
        o_ref[:, cols] = _dot(h_ref[...], w_buf[slot]).astype(o_ref.dtype)

    _stream_weight_tiles(o_ref.shape[1] // tn, copies, consume)


def _in_proj(x2, mod3, b0, seq, norm_g, w_bf, wg_bf, tm=512, tn=512):
    m, d = x2.shape
    n = min(w_bf.shape[1], PROJ_WIDTH)
    assert n % tn == 0
    mod_spec = lambda k: pl.BlockSpec((1, 1, d), lambda i: ((b0 + (i * tm) // seq) * 6 + k, 0, 0))
    return pl.pallas_call(
        functools.partial(_inproj_kernel, tn=tn),
        grid=(m // tm,),
        in_specs=[pl.BlockSpec((tm, d), lambda i: (i, 0)),
                  mod_spec(0), mod_spec(1),
                  pl.BlockSpec((1, d), lambda i: (0, 0)),
                  pl.BlockSpec(memory_space=pl.ANY),
                  pl.BlockSpec((d, GATE_PAD), lambda i: (0, 0))],
        out_specs=[pl.BlockSpec((tm, n), lambda i: (i, 0)),
                   pl.BlockSpec((tm, GATE_PAD), lambda i: (i, 0))],
        out_shape=[jax.ShapeDtypeStruct((m, n), BF16),
                   jax.ShapeDtypeStruct((m, GATE_PAD), F32)],
        scratch_shapes=[pltpu.VMEM((tm, d), BF16), pltpu.VMEM((tm, LANES), F32),
                        pltpu.VMEM((2, SUBLANES, d), F32),
                        pltpu.VMEM((2, d, tn), BF16), pltpu.SemaphoreType.DMA((2,))],
        compiler_params=_cparams(1),
        name="in_proj",
    )(x2, mod3, mod3, norm_g, w_bf, wg_bf)


def _t5_bucket_table():
    half = N_BUCKETS // 2
    max_exact = half // 2
    qi = np.arange(BLOCK)[:, None]
    kj = np.arange(3 * BLOCK)[None, :]
    rel = kj - BLOCK - qi
    n = np.abs(rel)
    nf = np.maximum(n, 1).astype(np.float32)
    large = max_exact + (np.log(nf / max_exact) / math.log(MAX_DISTANCE / max_exact)
                         * (half - max_exact)).astype(np.int32)
    large = np.minimum(large, half - 1)
    return (np.where(rel > 0, half, 0) + np.where(n < max_exact, n, large)).astype(np.int32)


def _attn_kernel(rb_ref, sink_ref, bucket_ref, q_ref, k_ref, v_ref, o_ref, tbl_ref, *, nb):
    b = pl.program_id(0)
    n = pl.program_id(1)

    @pl.when((b == 0) & (n == 0))
    def _():
        qi = lax.broadcasted_iota(jnp.int32, (BLOCK, 3 * BLOCK), 0)
        kj = lax.broadcasted_iota(jnp.int32, (BLOCK, 3 * BLOCK), 1)
        in_band = jnp.abs(kj - BLOCK - qi) <= WINDOW
        bucket = bucket_ref[...]

        def per_head(h, carry):
            def per_bucket(bk, acc):
                return jnp.where(bucket == bk, rb_ref[bk, h], acc)
            acc = lax.fori_loop(0, N_BUCKETS, per_bucket, jnp.zeros((BLOCK, 3 * BLOCK), F32)) * LOG2E
            for v in range(4):
                keep = in_band
                if v & 1:
                    keep = keep & (kj >= BLOCK)
                if v & 2:
                    keep = keep & (kj < 2 * BLOCK)
                tbl_ref[v, h] = jnp.where(keep, acc, NEG_INF)
            return carry

        lax.fori_loop(0, N_ATTN_HEADS, per_head, 0)

    r_prev = pl.multiple_of(jnp.maximum(n - 1, 0) * BLOCK, BLOCK)
    r_cur = pl.multiple_of(n * BLOCK, BLOCK)
    r_next = pl.multiple_of(jnp.minimum(n + 1, nb - 1) * BLOCK, BLOCK)
    variant = (n == 0).astype(jnp.int32) + 2 * (n == nb - 1).astype(jnp.int32)
    scale2 = HEAD_DIM ** -0.5 * LOG2E

    for g in range(N_KV_HEADS):
        kc = slice(g * HEAD_DIM, (g + 1) * HEAD_DIM)
        k3 = jnp.concatenate([k_ref[0, pl.ds(r_prev, BLOCK), kc],
                              k_ref[0, pl.ds(r_cur, BLOCK), kc],
                              k_ref[0, pl.ds(r_next, BLOCK), kc]], axis=0)
        v3 = jnp.concatenate([v_ref[0, pl.ds(r_prev, BLOCK), kc],
                              v_ref[0, pl.ds(r_cur, BLOCK), kc],
                              v_ref[0, pl.ds(r_next, BLOCK), kc]], axis=0)
        heads = [g * GQA_GROUP + j for j in range(GQA_GROUP)]
        qs = jnp.concatenate([q_ref[0, :, h * HEAD_DIM:(h + 1) * HEAD_DIM] for h in heads], axis=0)
        s = _dot_nt(qs, k3)
        ps, inv_dens = [], []
        for j, h in enumerate(heads):
            sj = s[j * BLOCK:(j + 1) * BLOCK] * scale2 + tbl_ref[variant, h]
            sink = sink_ref[0, h] * LOG2E
            m = jnp.maximum(jnp.max(sj, axis=-1, keepdims=True), sink)
            p = jnp.exp2(sj - m)
            den = jnp.sum(p, axis=-1, keepdims=True) + jnp.exp2(sink - m)
            ps.append(p.astype(BF16))
            inv_dens.append(1.0 / den)
        o = _dot(jnp.concatenate(ps, axis=0), v3)
        for j, h in enumerate(heads):
            o_ref[0, :, h * HEAD_DIM:(h + 1) * HEAD_DIM] = (
                o[j * BLOCK:(j + 1) * BLOCK] * inv_dens[j]).astype(o_ref.dtype)


def _attention(proj3, rel_bias, sink, bucket):
    bsz, seq, _ = proj3.shape
    nb = seq // BLOCK
    smem = pl.BlockSpec(memory_space=pltpu.SMEM)
    return pl.pallas_call(
        functools.partial(_attn_kernel, nb=nb),
        grid=(bsz, nb),
        in_specs=[smem, smem,
                  pl.BlockSpec((BLOCK, 3 * BLOCK), lambda b, n: (0, 0)),
                  pl.BlockSpec((1, BLOCK, ATTN_WIDTH), lambda b, n: (b, n, OFF_QA // ATTN_WIDTH)),
                  pl.BlockSpec((1, seq, KV_WIDTH), lambda b, n: (b, 0, OFF_KA // KV_WIDTH)),
                  pl.BlockSpec((1, seq, KV_WIDTH), lambda b, n: (b, 0, OFF_VA // KV_WIDTH))],
        out_specs=pl.BlockSpec((1, BLOCK, ATTN_WIDTH), lambda b, n: (b, n, 0)),
        out_shape=jax.ShapeDtypeStruct((bsz, seq, ATTN_WIDTH), BF16),
        scratch_shapes=[pltpu.VMEM((4, N_ATTN_HEADS, BLOCK, 3 * BLOCK), F32)],
        compiler_params=_cparams(2),
        name="window_attn",
    )(rel_bias, sink, bucket, proj3, proj3, proj3)


def _gla_tri_mats():
    t = np.arange(GLA_BLOCK)
    same = (t[:, None] // GLA_CHUNK) == (t[None, :] // GLA_CHUNK)
    fwd = same & (t[None, :] <= t[:, None])
    bwd = same & (t[None, :] >= t[:, None])
    return np.stack([fwd, bwd]).astype(np.float32)


def _gla_kernel(tri_ref, msk_ref, q_ref, k_ref, v_ref, r_ref, gt_ref, wa_ref, ba_ref, ng_ref, o_ref,
                bcum_ref, qt_ref, stb_ref, st_ref, u_ref, acc_ref, *, seq):
    c = GLA_CHUNK
    cpb = GLA_BLOCK // c
    nblk = seq // GLA_BLOCK
    scale = GLA_DK ** -0.5
    anchor = (c // 2 - 1, c // 2)
    last = (c - 1, 0)

    def gates(d, bi):
        rows = pl.ds(pl.multiple_of(bi * GLA_BLOCK, GLA_BLOCK), GLA_BLOCK)
        z = _dot(gt_ref[0, rows, :].astype(BF16), wa_ref[d]) + ba_ref[d]
        lg = (jnp.minimum(z, 0.0) - jnp.log(1.0 + jnp.exp(-jnp.abs(z)))) * (1.0 / GATE_TEMP)
        hi = lg.astype(BF16)
        lo = (lg - hi.astype(F32)).astype(BF16)
        cum = _dot(tri_ref[d], jnp.concatenate([hi, lo], axis=1))
        bcum_ref[d, rows, :] = cum[:, :GLA_DK] + cum[:, GLA_DK:]

    st_ref[...] = jnp.zeros_like(st_ref)
    for sub in range(GLA_UNROLL):
        gates(0, sub)
        gates(1, nblk - 1 - sub)

    def scan_blocks(i, carry):
        for sub in range(GLA_UNROLL):
            ib = i * GLA_UNROLL + sub
            for d, bi in ((0, ib), (1, nblk - 1 - ib)):
                r0 = pl.multiple_of(bi * GLA_BLOCK, GLA_BLOCK)
                lanes = slice(d * GLA_DK, (d + 1) * GLA_DK)
                decays, qps, kps = [], [], []
                for ci in range(cpb):
                    rows = pl.ds(r0 + ci * c, c)
                    bc = bcum_ref[d, rows, :]
                    b_a = bc[anchor[d]:anchor[d] + 1]
                    b_l = bc[last[d]:last[d] + 1]
                    qpf = q_ref[0, rows, :].astype(F32) * scale * jnp.exp(bc - b_a)
                    kpf = k_ref[0, rows, :].astype(F32) * jnp.exp(b_a - bc)
                    qps.append(qpf.astype(BF16))
                    kps.append(kpf.astype(BF16))
                    qt_ref[rows, lanes] = (qpf * jnp.exp(b_a)).astype(BF16)
                    kd = (kpf * jnp.exp(b_l - b_a)).astype(BF16)
                    u_ref[d, sub * cpb + ci] = _dot_tn(v_ref[0, rows, :], kd)
                    decays.append(jnp.exp(b_l))
                rows = pl.ds(r0, GLA_BLOCK)
                s = _dot_nt(jnp.concatenate(qps, axis=0), jnp.concatenate(kps, axis=0))
                a = jnp.where(msk_ref[d] > 0.5, s, 0.0).astype(BF16)
                acc_ref[d, rows, :] = _dot(a, v_ref[0, rows, :])
                for s in range(GLA_DV // GLA_SLAB):
                    sl = slice(s * GLA_SLAB, (s + 1) * GLA_SLAB)
                    st = st_ref[d, sl, :]
                    for ci in (range(cpb) if d == 0 else reversed(range(cpb))):
                        stb_ref[bi * cpb + ci, sl, lanes] = st.astype(BF16)
                        st = st * decays[ci] + u_ref[d, sub * cpb + ci, sl, :]
                    st_ref[d, sl, :] = st
        for sub in range(GLA_UNROLL):
            ib = (i + 1) * GLA_UNROLL + sub
            gates(0, jnp.minimum(ib, nblk - 1))
            gates(1, jnp.maximum(nblk - 1 - ib, 0))
        return carry

    lax.fori_loop(0, nblk // GLA_UNROLL, scan_blocks, 0)

    ng = ng_ref[...]

    def post(r0):
        for ci in range(2 * cpb):
            rows = pl.ds(r0 + ci * c, c)
            o = (acc_ref[0, rows, :] + acc_ref[1, rows, :]
                 + _dot_nt(qt_ref[rows, :], stb_ref[lax.div(r0, c) + ci]))
            ms = jnp.sum(o * o, axis=-1, keepdims=True) * (1.0 / GLA_DV)
            on = o * lax.rsqrt(ms + EPS) * ng
            o_ref[0, rows, :] = (on * _silu(r_ref[0, rows, :].astype(F32))).astype(o_ref.dtype)

    _row_loop(seq, 2 * GLA_BLOCK, post)


def _gla(proj3, gates3, tri, wa_pad, ba, norm_g):
    bsz, seq, _ = proj3.shape
    nc = seq // GLA_CHUNK
    whole = lambda shape: pl.BlockSpec(shape, lambda b, h: (0,) * len(shape))
    return pl.pallas_call(
        functools.partial(_gla_kernel, seq=seq),
        grid=(bsz, N_GLA_HEADS),
        in_specs=[whole((2, GLA_BLOCK, GLA_BLOCK)),
                  whole((2, GLA_BLOCK, GLA_BLOCK)),
                  pl.BlockSpec((1, seq, GLA_DK), lambda b, h: (b, 0, OFF_QG // GLA_DK + h)),
                  pl.BlockSpec((1, seq, GLA_DK), lambda b, h: (b, 0, OFF_KG // GLA_DK + h)),
                  pl.BlockSpec((1, seq, GLA_DV), lambda b, h: (b, 0, OFF_VG // GLA_DV + h)),
                  pl.BlockSpec((1, seq, GLA_DV), lambda b, h: (b, 0, OFF_RG // GLA_DV + h)),
                  pl.BlockSpec((1, seq, GATE_PAD), lambda b, h: (b, 0, 0)),
                  pl.BlockSpec((2, GATE_PAD, GLA_DK), lambda b, h: (0, 0, h)),
                  pl.BlockSpec((2, 1, GLA_DK), lambda b, h: (0, 0, h)),
                  whole((1, GLA_DV))],
        out_specs=pl.BlockSpec((1, seq, GLA_DV), lambda b, h: (b, 0, h)),
        out_shape=jax.ShapeDtypeStruct((bsz, seq, GLA_V_WIDTH), BF16),
        scratch_shapes=[pltpu.VMEM((2, seq, GLA_DK), F32),
                        pltpu.VMEM((seq, 2 * GLA_DK), BF16),
                        pltpu.VMEM((nc, GLA_DV, 2 * GLA_DK), BF16),
                        pltpu.VMEM((2, GLA_DV, GLA_DK), F32),
                        pltpu.VMEM((2, GLA_UNROLL * GLA_BLOCK // GLA_CHUNK, GLA_DV, GLA_DK), F32),
                        pltpu.VMEM((2, seq, GLA_DV), F32)],
        compiler_params=_cparams(2),
        name="gla",
    )(tri.astype(BF16), tri, proj3, proj3, proj3, proj3, gates3, wa_pad, ba, norm_g)


def _outproj_kernel(a_ref, g_ref, wa_ref, wg_ref, x_ref, g1_ref, o_ref):
    acc = _dot(a_ref[...], wa_ref[...]) + _dot(g_ref[...], wg_ref[...])
    o_ref[...] = x_ref[...] + g1_ref[0] * acc


def _out_proj(attn2, gla2, w_bf, x2, mod3, b0, seq, tm=512, tn=1024):
    m, d = x2.shape
    ka = attn2.shape[1]
    kg = gla2.shape[1]
    return pl.pallas_call(
        _outproj_kernel,
        grid=(m // tm, d // tn),
        in_specs=[pl.BlockSpec((tm, ka), lambda i, j: (i, 0)),
                  pl.BlockSpec((tm, kg), lambda i, j: (i, 0)),
                  pl.BlockSpec((ka, tn), lambda i, j: (0, j)),
                  pl.BlockSpec((kg, tn), lambda i, j: (ka // kg, j)),
                  pl.BlockSpec((tm, tn), lambda i, j: (i, j)),
                  pl.BlockSpec((1, 1, tn), lambda i, j: ((b0 + (i * tm) // seq) * 6 + 2, 0, j))],
        out_specs=pl.BlockSpec((tm, tn), lambda i, j: (i, j)),
        out_shape=jax.ShapeDtypeStruct((m, d), F32),
        compiler_params=_cparams(2),
        name="out_proj",
    )(attn2, gla2, w_bf, w_bf, x2, mod3)


def _mlp_kernel(x_ref, sh_ref, sc_ref, g2_ref, ng_ref, fg_ref, w1_hbm, w2_hbm, o_ref,
                h_ref, r_ref, m_ref, w1_buf, w2_buf, sem, *, tf, tn):
    tm, d = x_ref.shape

    _modulated_rms_rows(x_ref, h_ref, r_ref, m_ref, ng_ref[...], sc_ref[0], sh_ref[0], tm)
    o_ref[...] = jnp.zeros_like(o_ref)

    def copies(f, slot):
        return (pltpu.make_async_copy(w1_hbm.at[:, pl.ds(f * tf, tf)], w1_buf.at[slot], sem.at[0, slot]),
                pltpu.make_async_copy(w2_hbm.at[pl.ds(f * tf, tf), :], w2_buf.at[slot], sem.at[1, slot]))

    def consume(f, slot):
        u = _dot(h_ref[...], w1_buf[slot])
        u = jnp.square(jnp.maximum(u, 0.0)).astype(BF16)
        for c in range(d // tn):
            cols = slice(c * tn, (c + 1) * tn)
            o_ref[:, cols] += _dot(u, w2_buf[slot, :, cols])

    _stream_weight_tiles(w1_hbm.shape[1] // tf, copies, consume)

    m_ref[0] = jnp.broadcast_to(fg_ref[...], (SUBLANES, d))
    m_ref[1] = jnp.broadcast_to(g2_ref[0], (SUBLANES, d))

    def residual(rows, cols):
        x2 = x_ref[rows, cols] + m_ref[1, :, cols] * o_ref[rows, cols]
        o_ref[rows, cols] = x2
        return x2

    _row_rsqrt(o_ref, r_ref, tm, load=residual)
    _scale_rows(o_ref, o_ref, r_ref, m_ref, tm, shift=False)


def _mlp(x1, mod3, b0, seq, norm_g, final_g, w1_bf, w2_bf, tm=512, tf=512, tn=512):
    m, d = x1.shape
    dff = w1_bf.shape[1]
    mod_spec = lambda k: pl.BlockSpec((1, 1, d), lambda i: ((b0 + (i * tm) // seq) * 6 + k, 0, 0))
    return pl.pallas_call(
        functools.partial(_mlp_kernel, tf=tf, tn=tn),
        grid=(m // tm,),
        in_specs=[pl.BlockSpec((tm, d), lambda i: (i, 0)),
                  mod_spec(3), mod_spec(4), mod_spec(5),
                  pl.BlockSpec((1, d), lambda i: (0, 0)),
                  pl.BlockSpec((1, d), lambda i: (0, 0)),
                  pl.BlockSpec(memory_space=pl.ANY),
                  pl.BlockSpec(memory_space=pl.ANY)],
        out_specs=pl.BlockSpec((tm, d), lambda i: (i, 0)),
        out_shape=jax.ShapeDtypeStruct((m, d), F32),
        scratch_shapes=[pltpu.VMEM((tm, d), BF16), pltpu.VMEM((tm, LANES), F32),
                        pltpu.VMEM((2, SUBLANES, d), F32),
                        pltpu.VMEM((2, d, tf), BF16), pltpu.VMEM((2, tf, d), BF16),
                        pltpu.SemaphoreType.DMA((2, 2))],
        compiler_params=_cparams(1),
        name="mlp",
    )(x1, mod3, mod3, mod3, norm_g, final_g, w1_bf, w2_bf)


def _trunk_group(x, mod3, b0, p):
    bsz, seq, d = x.shape
    x2 = x.reshape(bsz * seq, d)
    proj, gates = _in_proj(x2, mod3, b0, seq, p["norm1_g"], p["w_in"], p["w_gate"])
    proj3 = proj.reshape(bsz, seq, PROJ_WIDTH)
    gates3 = gates.reshape(bsz, seq, GATE_PAD)
    attn = _attention(proj3, p["rel_bias"], p["sink"], p["bucket"])
    gla = _gla(proj3, gates3, p["tri"], p["wa_pad"], p["ba"], p["gla_norm_g"])
    x1 = _out_proj(attn.reshape(bsz * seq, ATTN_WIDTH), gla.reshape(bsz * seq, GLA_V_WIDTH),
                   p["w_out"], x2, mod3, b0, seq)
    y = _mlp(x1, mod3, b0, seq, p["norm2_g"], p["final_g"], p["w_mlp_in"], p["w_mlp_out"])
    return y.reshape(bsz, seq, d)


def kernel(x_prompt, x_sample, c_prompt, c_sample, w_ada, b_ada, norm1_g, w_in, gla_wa_fwd, gla_ba_fwd, gla_wa_bwd, gla_ba_bwd, gla_norm_g, attn_sink, rel_bias, w_out, norm2_g, w_mlp_in, w_mlp_out, final_g):
    assert w_ada.shape[0] == 1, "single-layer trunk"
    d = x_prompt.shape[-1]
    bp, bs = c_prompt.shape[0], c_sample.shape[0]

    rows = -(-(bp + bs) // 16) * 16
    c_pad = jnp.zeros((rows, d), F32).at[:bp].set(c_prompt).at[bp:bp + bs].set(c_sample)
    mod = _ada(c_pad, w_ada[0], b_ada[0])
    mod3 = mod.reshape(rows * 6, 1, d)

    w_in0 = w_in[0]
    wa_pad = jnp.zeros((2, GATE_PAD, GLA_K_WIDTH), F32)
    wa_pad = wa_pad.at[0, :GATE_RANK].set(gla_wa_fwd[0]).at[1, GATE_RANK:2 * GATE_RANK].set(gla_wa_bwd[0])
    p = {
        "norm1_g": norm1_g[0].reshape(1, d),
        "norm2_g": norm2_g[0].reshape(1, d),
        "final_g": final_g.reshape(1, d),
        "w_in": w_in0.astype(BF16),
        "w_gate": jnp.pad(w_in0[:, OFF_GATE:], ((0, 0), (0, GATE_PAD - 2 * GATE_RANK))).astype(BF16),
        "w_out": w_out[0].astype(BF16),
        "w_mlp_in": w_mlp_in[0].astype(BF16),
        "w_mlp_out": w_mlp_out[0].astype(BF16),
        "wa_pad": wa_pad.astype(BF16),
        "ba": jnp.stack([gla_ba_fwd[0], gla_ba_bwd[0]]).reshape(2, 1, GLA_K_WIDTH),
        "gla_norm_g": gla_norm_g[0].reshape(1, GLA_DV),
        "rel_bias": rel_bias,
        "sink": attn_sink[0].reshape(1, N_ATTN_HEADS),
        "bucket": jnp.asarray(_t5_bucket_table()),
        "tri": jnp.asarray(_gla_tri_mats()),
    }
    y_prompt = _trunk_group(x_prompt, mod3, 0, p)
    y_sample = _trunk_group(x_sample, mod3, bp, p)
    return (y_prompt, y_sample)
```

```python
import functools
import math

import numpy as np
import jax
import jax.numpy as jnp
from jax import lax
from jax.experimental import pallas as pl
from jax.experimental.pallas import tpu as pltpu

F32 = jnp.float32
BF16 = jnp.bfloat16

HEAD_DIM = 128
N_ATTN_HEADS = 16
N_KV_HEADS = 4
GQA_GROUP = N_ATTN_HEADS // N_KV_HEADS
ATTN_WIDTH = N_ATTN_HEADS * HEAD_DIM
KV_WIDTH = N_KV_HEADS * HEAD_DIM
WINDOW = 128
BLOCK = 128
GLA_DV = 256
GLA_DK = 128
N_GLA_HEADS = 8
GLA_K_WIDTH = N_GLA_HEADS * GLA_DK
GLA_V_WIDTH = N_GLA_HEADS * GLA_DV
GATE_RANK = 16
GATE_TEMP = 16.0
N_BUCKETS = 32
MAX_DISTANCE = 128
EPS = 1e-6
NEG_INF = -1e30
LOG2E = math.log2(math.e)

OFF_QA = 0
OFF_KA = OFF_QA + ATTN_WIDTH
OFF_VA = OFF_KA + KV_WIDTH
OFF_QG = OFF_VA + KV_WIDTH
OFF_KG = OFF_QG + GLA_K_WIDTH
OFF_VG = OFF_KG + GLA_K_WIDTH
OFF_RG = OFF_VG + GLA_V_WIDTH
OFF_GATE = OFF_RG + GLA_V_WIDTH
PROJ_WIDTH = OFF_GATE
GATE_PAD = 128

GLA_CHUNK = 64
GLA_BLOCK = 256
GLA_SLAB = 64
GLA_UNROLL = 2
INPROJ_SLOTS = 3
SUBLANES = 8
LANES = 128
NORM_ROWS = 16
RSQRT_GROUPS = 8

VMEM_LIMIT = 60 * 1024 * 1024


def _cparams(n_axes, vmem=VMEM_LIMIT):
    return pltpu.CompilerParams(dimension_semantics=("arbitrary",) * n_axes,
                                vmem_limit_bytes=vmem)


def _dot(a, b):
    return jnp.dot(a, b, preferred_element_type=F32)


def _dot_nt(a, b):
    return lax.dot_general(a, b, (((1,), (1,)), ((), ())), preferred_element_type=F32)


def _dot_tn(a, b):
    return lax.dot_general(a, b, (((0,), (0,)), ((), ())), preferred_element_type=F32)


def _silu(x):
    return x * (1.0 / (1.0 + jnp.exp(-x)))


def _row_loop(n_rows, rows_per, body):
    def step(i, carry):
        body(pl.multiple_of(i * rows_per, rows_per))
        return carry
    lax.fori_loop(0, n_rows // rows_per, step, 0)


def _ada_kernel(c_ref, w_ref, b_ref, o_ref):
    a = _silu(c_ref[...]).astype(BF16)
    o_ref[...] = _dot(a, w_ref[...].astype(BF16)) + b_ref[...]


def _ada(c_pad, w_ada, b_ada, tn=512):
    rows, d = c_pad.shape
    n = w_ada.shape[1]
    return pl.pallas_call(
        _ada_kernel,
        grid=(n // tn,),
        in_specs=[pl.BlockSpec((rows, d), lambda j: (0, 0)),
                  pl.BlockSpec((d, tn), lambda j: (0, j)),
                  pl.BlockSpec((1, tn), lambda j: (0, j))],
        out_specs=pl.BlockSpec((rows, tn), lambda j: (0, j)),
        out_shape=jax.ShapeDtypeStruct((rows, n), F32),
        compiler_params=_cparams(1),
        name="ada",
    )(c_pad, w_ada, b_ada.reshape(1, n))


def _row_rsqrt(x_ref, r_ref, n_rows, load=None):
    d = x_ref.shape[-1]

    def body(r0):
        for sub in range(RSQRT_GROUPS):
            rows = pl.ds(r0 + sub * SUBLANES, SUBLANES)
            acc = jnp.zeros((SUBLANES, LANES), F32)
            for j in range(d // LANES):
                cols = slice(j * LANES, (j + 1) * LANES)
                x = x_ref[rows, cols] if load is None else load(rows, cols)
                acc = acc + x * x
            r_ref[rows, :] = acc

    _row_loop(n_rows, RSQRT_GROUPS * SUBLANES, body)
    ms = jnp.sum(r_ref[...], axis=-1, keepdims=True) * (1.0 / d)
    r_ref[...] = jnp.broadcast_to(lax.rsqrt(ms + EPS), r_ref.shape)


def _scale_rows(x_ref, o_ref, r_ref, m_ref, n_rows, shift):
    d = x_ref.shape[-1]
    reps = NORM_ROWS // SUBLANES

    def body(r0):
        rows = pl.ds(r0, NORM_ROWS)
        r = r_ref[rows, :]
        for j in range(d // LANES):
            cols = slice(j * LANES, (j + 1) * LANES)
            y = x_ref[rows, cols] * r * jnp.concatenate([m_ref[0, :, cols]] * reps, axis=0)
            if shift:
                y = y + jnp.concatenate([m_ref[1, :, cols]] * reps, axis=0)
            o_ref[rows, cols] = y.astype(o_ref.dtype)

    _row_loop(n_rows, NORM_ROWS, body)


def _modulated_rms_rows(x_ref, h_ref, r_ref, m_ref, g, sc, sh, n_rows):
    d = x_ref.shape[-1]
    m_ref[0] = jnp.broadcast_to(g * (1.0 + sc), (SUBLANES, d))
    m_ref[1] = jnp.broadcast_to(sh, (SUBLANES, d))
    _row_rsqrt(x_ref, r_ref, n_rows)
    _scale_rows(x_ref, h_ref, r_ref, m_ref, n_rows, shift=True)


def _stream_weight_tiles(n_tiles, n_slots, copies, consume):
    assert n_tiles % n_slots == 0, "one loop iteration consumes one tile per slot"
    ahead = n_slots - 1

    @pl.when(pl.program_id(0) == 0)
    def _():
        for t in range(ahead):
            for cp in copies(t, t):
                cp.start()

    def ring(k, carry):
        for slot in range(n_slots):
            t = n_slots * k + slot
            for cp in copies(t, slot):
                cp.wait()
            t_next = jnp.where(t + ahead >= n_tiles, t + ahead - n_tiles, t + ahead)
            for cp in copies(t_next, (slot + ahead) % n_slots):
                cp.start()
            consume(t, slot)
        return carry

    lax.fori_loop(0, n_tiles // n_slots, ring, 0)

    @pl.when(pl.program_id(0) == pl.num_programs(0) - 1)
    def _():
        for t in range(ahead):
            for cp in copies(t, t):
                cp.wait()


def _inproj_kernel(x_ref, sh_ref, sc_ref, g_ref, w_hbm, wg_ref, o_ref, og_ref,
                   h_ref, r_ref, m_ref, w_buf, sem, *, tn):
    _modulated_rms_rows(x_ref, h_ref, r_ref, m_ref, g_ref[...], sc_ref[0], sh_ref[0], x_ref.shape[0])
    og_ref[...] = _dot(h_ref[...], wg_ref[...])

    def copies(t, slot):
        return (pltpu.make_async_copy(w_hbm.at[:, pl.ds(t * tn, tn)], w_buf.at[slot], sem.at[slot]),)

    def consume(t, slot):
        cols = pl.ds(pl.multiple_of(t * tn, tn), tn)
        o_ref[:, cols] = _dot(h_ref[...], w_buf[slot]).astype(o_ref.dtype)

    _stream_weight_tiles(o_ref.shape[1] // tn, w_buf.shape[0], copies, consume)


def _in_proj(x2, mod3, b0, seq, norm_g, w_bf, wg_bf, tm=512, tn=512):
    m, d = x2.shape
    n = min(w_bf.shape[1], PROJ_WIDTH)
    assert n % tn == 0
    mod_spec = lambda k: pl.BlockSpec((1, 1, d), lambda i: ((b0 + (i * tm) // seq) * 6 + k, 0, 0))
    return pl.pallas_call(
        functools.partial(_inproj_kernel, tn=tn),
        grid=(m // tm,),
        in_specs=[pl.BlockSpec((tm, d), lambda i: (i, 0)),
                  mod_spec(0), mod_spec(1),
                  pl.BlockSpec((1, d), lambda i: (0, 0)),
                  pl.BlockSpec(memory_space=pl.ANY),
                  pl.BlockSpec((d, GATE_PAD), lambda i: (0, 0))],
        out_specs=[pl.BlockSpec((tm, n), lambda i: (i, 0)),
                   pl.BlockSpec((tm, GATE_PAD), lambda i: (i, 0))],
        out_shape=[jax.ShapeDtypeStruct((m, n), BF16),
                   jax.ShapeDtypeStruct((m, GATE_PAD), F32)],
        scratch_shapes=[pltpu.VMEM((tm, d), BF16), pltpu.VMEM((tm, LANES), F32),
                        pltpu.VMEM((2, SUBLANES, d), F32),
                        pltpu.VMEM((INPROJ_SLOTS, d, tn), BF16), pltpu.SemaphoreType.DMA((INPROJ_SLOTS,))],
        compiler_params=_cparams(1),
        name="in_proj",
    )(x2, mod3, mod3, norm_g, w_bf, wg_bf)


def _t5_bucket_table():
    half = N_BUCKETS // 2
    max_exact = half // 2
    qi = np.arange(BLOCK)[:, None]
    kj = np.arange(3 * BLOCK)[None, :]
    rel = kj - BLOCK - qi
    n = np.abs(rel)
    nf = np.maximum(n, 1).astype(np.float32)
    large = max_exact + (np.log(nf / max_exact) / math.log(MAX_DISTANCE / max_exact)
                         * (half - max_exact)).astype(np.int32)
    large = np.minimum(large, half - 1)
    return (np.where(rel > 0, half, 0) + np.where(n < max_exact, n, large)).astype(np.int32)


def _attn_kernel(rb_ref, sink_ref, bucket_ref, q_ref, k_ref, v_ref, o_ref, tbl_ref, *, nb):
    b = pl.program_id(0)
    n = pl.program_id(1)

    @pl.when((b == 0) & (n == 0))
    def _():
        qi = lax.broadcasted_iota(jnp.int32, (BLOCK, 3 * BLOCK), 0)
        kj = lax.broadcasted_iota(jnp.int32, (BLOCK, 3 * BLOCK), 1)
        in_band = jnp.abs(kj - BLOCK - qi) <= WINDOW
        bucket = bucket_ref[...]

        def per_head(h, carry):
            def per_bucket(bk, acc):
                return jnp.where(bucket == bk, rb_ref[bk, h], acc)
            acc = lax.fori_loop(0, N_BUCKETS, per_bucket, jnp.zeros((BLOCK, 3 * BLOCK), F32)) * LOG2E
            for v in range(4):
                keep = in_band
                if v & 1:
                    keep = keep & (kj >= BLOCK)
                if v & 2:
                    keep = keep & (kj < 2 * BLOCK)
                tbl_ref[v, h] = jnp.where(keep, acc, NEG_INF)
            return carry

        lax.fori_loop(0, N_ATTN_HEADS, per_head, 0)

    r_prev = pl.multiple_of(jnp.maximum(n - 1, 0) * BLOCK, BLOCK)
    r_cur = pl.multiple_of(n * BLOCK, BLOCK)
    r_next = pl.multiple_of(jnp.minimum(n + 1, nb - 1) * BLOCK, BLOCK)
    variant = (n == 0).astype(jnp.int32) + 2 * (n == nb - 1).astype(jnp.int32)
    scale2 = HEAD_DIM ** -0.5 * LOG2E

    for g in range(N_KV_HEADS):
        kc = slice(g * HEAD_DIM, (g + 1) * HEAD_DIM)
        k3 = jnp.concatenate([k_ref[0, pl.ds(r_prev, BLOCK), kc],
                              k_ref[0, pl.ds(r_cur, BLOCK), kc],
                              k_ref[0, pl.ds(r_next, BLOCK), kc]], axis=0)
        v3 = jnp.concatenate([v_ref[0, pl.ds(r_prev, BLOCK), kc],
                              v_ref[0, pl.ds(r_cur, BLOCK), kc],
                              v_ref[0, pl.ds(r_next, BLOCK), kc]], axis=0)
        heads = [g * GQA_GROUP + j for j in range(GQA_GROUP)]
        qs = jnp.concatenate([q_ref[0, :, h * HEAD_DIM:(h + 1) * HEAD_DIM] for h in heads], axis=0)
        s = _dot_nt(qs, k3)
        ps, inv_dens = [], []
        for j, h in enumerate(heads):
            sj = s[j * BLOCK:(j + 1) * BLOCK] * scale2 + tbl_ref[variant, h]
            sink = sink_ref[0, h] * LOG2E
            m = jnp.maximum(jnp.max(sj, axis=-1, keepdims=True), sink)
            p = jnp.exp2(sj - m)
            den = jnp.sum(p, axis=-1, keepdims=True) + jnp.exp2(sink - m)
            ps.append(p.astype(BF16))
            inv_dens.append(1.0 / den)
        o = _dot(jnp.concatenate(ps, axis=0), v3)
        for j, h in enumerate(heads):
            o_ref[0, :, h * HEAD_DIM:(h + 1) * HEAD_DIM] = (
                o[j * BLOCK:(j + 1) * BLOCK] * inv_dens[j]).astype(o_ref.dtype)


def _attention(proj3, rel_bias, sink, bucket):
    bsz, seq, _ = proj3.shape
    nb = seq // BLOCK
    smem = pl.BlockSpec(memory_space=pltpu.SMEM)
    return pl.pallas_call(
        functools.partial(_attn_kernel, nb=nb),
        grid=(bsz, nb),
        in_specs=[smem, smem,
                  pl.BlockSpec((BLOCK, 3 * BLOCK), lambda b, n: (0, 0)),
                  pl.BlockSpec((1, BLOCK, ATTN_WIDTH), lambda b, n: (b, n, OFF_QA // ATTN_WIDTH)),
                  pl.BlockSpec((1, seq, KV_WIDTH), lambda b, n: (b, 0, OFF_KA // KV_WIDTH)),
                  pl.BlockSpec((1, seq, KV_WIDTH), lambda b, n: (b, 0, OFF_VA // KV_WIDTH))],
        out_specs=pl.BlockSpec((1, BLOCK, ATTN_WIDTH), lambda b, n: (b, n, 0)),
        out_shape=jax.ShapeDtypeStruct((bsz, seq, ATTN_WIDTH), BF16),
        scratch_shapes=[pltpu.VMEM((4, N_ATTN_HEADS, BLOCK, 3 * BLOCK), F32)],
        compiler_params=_cparams(2),
        name="window_attn",
    )(rel_bias, sink, bucket, proj3, proj3, proj3)


def _gla_tri_mats():
    t = np.arange(GLA_BLOCK)
    same = (t[:, None] // GLA_CHUNK) == (t[None, :] // GLA_CHUNK)
    fwd = same & (t[None, :] <= t[:, None])
    bwd = same & (t[None, :] >= t[:, None])
    return np.stack([fwd, bwd]).astype(np.float32)


def _gla_kernel(tri_ref, msk_ref, q_ref, k_ref, v_ref, r_ref, gt_ref, wa_ref, ba_ref, ng_ref, o_ref,
                bcum_ref, qt_ref, stb_ref, st_ref, u_ref, acc_ref, *, seq):
    c = GLA_CHUNK
    cpb = GLA_BLOCK // c
    nblk = seq // GLA_BLOCK
    scale = GLA_DK ** -0.5
    anchor = (c // 2 - 1, c // 2)
    last = (c - 1, 0)

    def gates(d, bi):
        rows = pl.ds(pl.multiple_of(bi * GLA_BLOCK, GLA_BLOCK), GLA_BLOCK)
        z = _dot(gt_ref[0, rows, :].astype(BF16), wa_ref[d]) + ba_ref[d]
        lg = (jnp.minimum(z, 0.0) - jnp.log(1.0 + jnp.exp(-jnp.abs(z)))) * (1.0 / GATE_TEMP)
        hi = lg.astype(BF16)
        lo = (lg - hi.astype(F32)).astype(BF16)
        cum = _dot(tri_ref[d], jnp.concatenate([hi, lo], axis=1))
        bcum_ref[d, rows, :] = cum[:, :GLA_DK] + cum[:, GLA_DK:]

    st_ref[...] = jnp.zeros_like(st_ref)
    for sub in range(GLA_UNROLL):
        gates(0, sub)
        gates(1, nblk - 1 - sub)

    def scan_blocks(i, carry):
        for sub in range(GLA_UNROLL):
            ib = i * GLA_UNROLL + sub
            for d, bi in ((0, ib), (1, nblk - 1 - ib)):
                r0 = pl.multiple_of(bi * GLA_BLOCK, GLA_BLOCK)
                lanes = slice(d * GLA_DK, (d + 1) * GLA_DK)
                decays, qps, kps = [], [], []
                for ci in range(cpb):
                    rows = pl.ds(r0 + ci * c, c)
                    bc = bcum_ref[d, rows, :]
                    b_a = bc[anchor[d]:anchor[d] + 1]
                    b_l = bc[last[d]:last[d] + 1]
                    qpf = q_ref[0, rows, :].astype(F32) * scale * jnp.exp(bc - b_a)
                    kpf = k_ref[0, rows, :].astype(F32) * jnp.exp(b_a - bc)
                    qps.append(qpf.astype(BF16))
                    kps.append(kpf.astype(BF16))
                    qt_ref[rows, lanes] = (qpf * jnp.exp(b_a)).astype(BF16)
                    kd = (kpf * jnp.exp(b_l - b_a)).astype(BF16)
                    u_ref[d, sub * cpb + ci] = _dot_tn(v_ref[0, rows, :], kd)
                    decays.append(jnp.exp(b_l))
                rows = pl.ds(r0, GLA_BLOCK)
                s = _dot_nt(jnp.concatenate(qps, axis=0), jnp.concatenate(kps, axis=0))
                a = jnp.where(msk_ref[d] > 0.5, s, 0.0).astype(BF16)
                acc_ref[d, rows, :] = _dot(a, v_ref[0, rows, :])
                for s in range(GLA_DV // GLA_SLAB):
                    sl = slice(s * GLA_SLAB, (s + 1) * GLA_SLAB)
                    st = st_ref[d, sl, :]
                    for ci in (range(cpb) if d == 0 else reversed(range(cpb))):
                        stb_ref[bi * cpb + ci, sl, lanes] = st.astype(BF16)
                        st = st * decays[ci] + u_ref[d, sub * cpb + ci, sl, :]
                    st_ref[d, sl, :] = st
        for sub in range(GLA_UNROLL):
            ib = (i + 1) * GLA_UNROLL + sub
            gates(0, jnp.minimum(ib, nblk - 1))
            gates(1, jnp.maximum(nblk - 1 - ib, 0))
        return carry

    lax.fori_loop(0, nblk // GLA_UNROLL, scan_blocks, 0)

    ng = ng_ref[...]

    def post(r0):
        for ci in range(2 * cpb):
            rows = pl.ds(r0 + ci * c, c)
            o = (acc_ref[0, rows, :] + acc_ref[1, rows, :]
                 + _dot_nt(qt_ref[rows, :], stb_ref[lax.div(r0, c) + ci]))
            ms = jnp.sum(o * o, axis=-1, keepdims=True) * (1.0 / GLA_DV)
            on = o * lax.rsqrt(ms + EPS) * ng
            o_ref[0, rows, :] = (on * _silu(r_ref[0, rows, :].astype(F32))).astype(o_ref.dtype)

    _row_loop(seq, 2 * GLA_BLOCK, post)


def _gla(proj3, gates3, tri, wa_pad, ba, norm_g):
    bsz, seq, _ = proj3.shape
    nc = seq // GLA_CHUNK
    whole = lambda shape: pl.BlockSpec(shape, lambda b, h: (0,) * len(shape))
    return pl.pallas_call(
        functools.partial(_gla_kernel, seq=seq),
        grid=(bsz, N_GLA_HEADS),
        in_specs=[whole((2, GLA_BLOCK, GLA_BLOCK)),
                  whole((2, GLA_BLOCK, GLA_BLOCK)),
                  pl.BlockSpec((1, seq, GLA_DK), lambda b, h: (b, 0, OFF_QG // GLA_DK + h)),
                  pl.BlockSpec((1, seq, GLA_DK), lambda b, h: (b, 0, OFF_KG // GLA_DK + h)),
                  pl.BlockSpec((1, seq, GLA_DV), lambda b, h: (b, 0, OFF_VG // GLA_DV + h)),
                  pl.BlockSpec((1, seq, GLA_DV), lambda b, h: (b, 0, OFF_RG // GLA_DV + h)),
                  pl.BlockSpec((1, seq, GATE_PAD), lambda b, h: (b, 0, 0)),
                  pl.BlockSpec((2, GATE_PAD, GLA_DK), lambda b, h: (0, 0, h)),
                  pl.BlockSpec((2, 1, GLA_DK), lambda b, h: (0, 0, h)),
                  whole((1, GLA_DV))],
        out_specs=pl.BlockSpec((1, seq, GLA_DV), lambda b, h: (b, 0, h)),
        out_shape=jax.ShapeDtypeStruct((bsz, seq, GLA_V_WIDTH), BF16),
        scratch_shapes=[pltpu.VMEM((2, seq, GLA_DK), F32),
                        pltpu.VMEM((seq, 2 * GLA_DK), BF16),
                        pltpu.VMEM((nc, GLA_DV, 2 * GLA_DK), BF16),
                        pltpu.VMEM((2, GLA_DV, GLA_DK), F32),
                        pltpu.VMEM((2, GLA_UNROLL * GLA_BLOCK // GLA_CHUNK, GLA_DV, GLA_DK), F32),
                        pltpu.VMEM((2, seq, GLA_DV), F32)],
        compiler_params=_cparams(2),
        name="gla",
    )(tri.astype(BF16), tri, proj3, proj3, proj3, proj3, gates3, wa_pad, ba, norm_g)


def _outproj_kernel(a_ref, g_ref, wa_ref, wg_ref, x_ref, g1_ref, o_ref):
    acc = _dot(a_ref[...], wa_ref[...]) + _dot(g_ref[...], wg_ref[...])
    o_ref[...] = x_ref[...] + g1_ref[0] * acc


def _out_proj(attn2, gla2, w_bf, x2, mod3, b0, seq, tm=1024, tn=1024):
    m, d = x2.shape
    ka = attn2.shape[1]
    kg = gla2.shape[1]
    return pl.pallas_call(
        _outproj_kernel,
        grid=(m // tm, d // tn),
        in_specs=[pl.BlockSpec((tm, ka), lambda i, j: (i, 0)),
                  pl.BlockSpec((tm, kg), lambda i, j: (i, 0)),
                  pl.BlockSpec((ka, tn), lambda i, j: (0, j)),
                  pl.BlockSpec((kg, tn), lambda i, j: (ka // kg, j)),
                  pl.BlockSpec((tm, tn), lambda i, j: (i, j)),
                  pl.BlockSpec((1, 1, tn), lambda i, j: ((b0 + (i * tm) // seq) * 6 + 2, 0, j))],
        out_specs=pl.BlockSpec((tm, tn), lambda i, j: (i, j)),
        out_shape=jax.ShapeDtypeStruct((m, d), F32),
        compiler_params=_cparams(2),
        name="out_proj",
    )(attn2, gla2, w_bf, w_bf, x2, mod3)


def _mlp_kernel(x_ref, sh_ref, sc_ref, g2_ref, ng_ref, fg_ref, w1_hbm, w2_hbm, o_ref,
                h_ref, r_ref, m_ref, w1_buf, w2_buf, sem, *, tf, tn):
    tm, d = x_ref.shape

    _modulated_rms_rows(x_ref, h_ref, r_ref, m_ref, ng_ref[...], sc_ref[0], sh_ref[0], tm)
    o_ref[...] = jnp.zeros_like(o_ref)

    def copies(f, slot):
        return (pltpu.make_async_copy(w1_hbm.at[:, pl.ds(f * tf, tf)], w1_buf.at[slot], sem.at[0, slot]),
                pltpu.make_async_copy(w2_hbm.at[pl.ds(f * tf, tf), :], w2_buf.at[slot], sem.at[1, slot]))

    def consume(f, slot):
        u = _dot(h_ref[...], w1_buf[slot])
        u = jnp.square(jnp.maximum(u, 0.0)).astype(BF16)
        for c in range(d // tn):
            cols = slice(c * tn, (c + 1) * tn)
            o_ref[:, cols] += _dot(u, w2_buf[slot, :, cols])

    _stream_weight_tiles(w1_hbm.shape[1] // tf, w1_buf.shape[0], copies, consume)

    m_ref[0] = jnp.broadcast_to(fg_ref[...], (SUBLANES, d))
    m_ref[1] = jnp.broadcast_to(g2_ref[0], (SUBLANES, d))

    def residual(rows, cols):
        x2 = x_ref[rows, cols] + m_ref[1, :, cols] * o_ref[rows, cols]
        o_ref[rows, cols] = x2
        return x2

    _row_rsqrt(o_ref, r_ref, tm, load=residual)
    _scale_rows(o_ref, o_ref, r_ref, m_ref, tm, shift=False)


def _mlp(x1, mod3, b0, seq, norm_g, final_g, w1_bf, w2_bf, tm=512, tf=512, tn=512):
    m, d = x1.shape
    dff = w1_bf.shape[1]
    mod_spec = lambda k: pl.BlockSpec((1, 1, d), lambda i: ((b0 + (i * tm) // seq) * 6 + k, 0, 0))
    return pl.pallas_call(
        functools.partial(_mlp_kernel, tf=tf, tn=tn),
        grid=(m // tm,),
        in_specs=[pl.BlockSpec((tm, d), lambda i: (i, 0)),
                  mod_spec(3), mod_spec(4), mod_spec(5),
                  pl.BlockSpec((1, d), lambda i: (0, 0)),
                  pl.BlockSpec((1, d), lambda i: (0, 0)),
                  pl.BlockSpec(memory_space=pl.ANY),
                  pl.BlockSpec(memory_space=pl.ANY)],
        out_specs=pl.BlockSpec((tm, d), lambda i: (i, 0)),
        out_shape=jax.ShapeDtypeStruct((m, d), F32),
        scratch_shapes=[pltpu.VMEM((tm, d), BF16), pltpu.VMEM((tm, LANES), F32),
                        pltpu.VMEM((2, SUBLANES, d), F32),
                        pltpu.VMEM((2, d, tf), BF16), pltpu.VMEM((2, tf, d), BF16),
                        pltpu.SemaphoreType.DMA((2, 2))],
        compiler_params=_cparams(1),
        name="mlp",
    )(x1, mod3, mod3, mod3, norm_g, final_g, w1_bf, w2_bf)


def _trunk_group(x, mod3, b0, p):
    bsz, seq, d = x.shape
    x2 = x.reshape(bsz * seq, d)
    proj, gates = _in_proj(x2, mod3, b0, seq, p["norm1_g"], p["w_in"], p["w_gate"])
    proj3 = proj.reshape(bsz, seq, PROJ_WIDTH)
    gates3 = gates.reshape(bsz, seq, GATE_PAD)
    attn = _attention(proj3, p["rel_bias"], p["sink"], p["bucket"])
    gla = _gla(proj3, gates3, p["tri"], p["wa_pad"], p["ba"], p["gla_norm_g"])
    x1 = _out_proj(attn.reshape(bsz * seq, ATTN_WIDTH), gla.reshape(bsz * seq, GLA_V_WIDTH),
                   p["w_out"], x2, mod3, b0, seq)
    y = _mlp(x1, mod3, b0, seq, p["norm2_g"], p["final_g"], p["w_mlp_in"], p["w_mlp_out"])
    return y.reshape(bsz, seq, d)


def kernel(x_prompt, x_sample, c_prompt, c_sample, w_ada, b_ada, norm1_g, w_in, gla_wa_fwd, gla_ba_fwd, gla_wa_bwd, gla_ba_bwd, gla_norm_g, attn_sink, rel_bias, w_out, norm2_g, w_mlp_in, w_mlp_out, final_g):
    assert w_ada.shape[0] == 1, "single-layer trunk"
    d = x_prompt.shape[-1]
    bp, bs = c_prompt.shape[0], c_sample.shape[0]

    rows = -(-(bp + bs) // 16) * 16
    c_pad = jnp.zeros((rows, d), F32).at[:bp].set(c_prompt).at[bp:bp + bs].set(c_sample)
    mod = _ada(c_pad, w_ada[0], b_ada[0])
    mod3 = mod.reshape(rows * 6, 1, d)

    w_in0 = w_in[0]
    wa_pad = jnp.zeros((2, GATE_PAD, GLA_K_WIDTH), F32)
    wa_pad = wa_pad.at[0, :GATE_RANK].set(gla_wa_fwd[0]).at[1, GATE_RANK:2 * GATE_RANK].set(gla_wa_bwd[0])
    p = {
        "norm1_g": norm1_g[0].reshape(1, d),
        "norm2_g": norm2_g[0].reshape(1, d),
        "final_g": final_g.reshape(1, d),
        "w_in": w_in0.astype(BF16),
        "w_gate": jnp.pad(w_in0[:, OFF_GATE:], ((0, 0), (0, GATE_PAD - 2 * GATE_RANK))).astype(BF16),
        "w_out": w_out[0].astype(BF16),
        "w_mlp_in": w_mlp_in[0].astype(BF16),
        "w_mlp_out": w_mlp_out[0].astype(BF16),
        "wa_pad": wa_pad.astype(BF16),
        "ba": jnp.stack([gla_ba_fwd[0], gla_ba_bwd[0]]).reshape(2, 1, GLA_K_WIDTH),
        "gla_norm_g": gla_norm_g[0].reshape(1, GLA_DV),
        "rel_bias": rel_bias,
        "sink": attn_sink[0].reshape(1, N_ATTN_HEADS),
        "bucket": jnp.asarray(_t5_bucket_table()),
        "tri": jnp.asarray(_gla_tri_mats()),
    }
    y_prompt = _trunk_group(x_prompt, mod3, 0, p)
    y_sample = _trunk_group(x_sample, mod3, bp, p)
    return (y_prompt, y_sample)
```

```python
import functools
import math

import numpy as np
import jax
import jax.numpy as jnp
from jax import lax
from jax.experimental import pallas as pl
from jax.experimental.pallas import tpu as pltpu

F32 = jnp.float32
BF16 = jnp.bfloat16

HEAD_DIM = 128
N_ATTN_HEADS = 16
N_KV_HEADS = 4
GQA_GROUP = N_ATTN_HEADS // N_KV_HEADS
ATTN_WIDTH = N_ATTN_HEADS * HEAD_DIM
KV_WIDTH = N_KV_HEADS * HEAD_DIM
WINDOW = 128
BLOCK = 128
GLA_DV = 256
GLA_DK = 128
N_GLA_HEADS = 8
GLA_K_WIDTH = N_GLA_HEADS * GLA_DK
GLA_V_WIDTH = N_GLA_HEADS * GLA_DV
GATE_RANK = 16
GATE_TEMP = 16.0
N_BUCKETS = 32
MAX_DISTANCE = 128
EPS = 1e-6
NEG_INF = -1e30
LOG2E = math.log2(math.e)

OFF_QA = 0
OFF_KA = OFF_QA + ATTN_WIDTH
OFF_VA = OFF_KA + KV_WIDTH
OFF_QG = OFF_VA + KV_WIDTH
OFF_KG = OFF_QG + GLA_K_WIDTH
OFF_VG = OFF_KG + GLA_K_WIDTH
OFF_RG = OFF_VG + GLA_V_WIDTH
OFF_GATE = OFF_RG + GLA_V_WIDTH
PROJ_WIDTH = OFF_GATE
GATE_PAD = 128

GLA_CHUNK = 64
GLA_BLOCK = 256
GLA_SLAB = 64
GLA_UNROLL = 2
INPROJ_SLOTS = 3
CAST_ROWS, CAST_COLS = 64, 4096
SUBLANES = 8
LANES = 128
NORM_ROWS = 16
RSQRT_GROUPS = 8

VMEM_LIMIT = 60 * 1024 * 1024


def _cparams(n_axes, vmem=VMEM_LIMIT):
    return pltpu.CompilerParams(dimension_semantics=("arbitrary",) * n_axes,
                                vmem_limit_bytes=vmem)


def _dot(a, b):
    return jnp.dot(a, b, preferred_element_type=F32)


def _dot_nt(a, b):
    return lax.dot_general(a, b, (((1,), (1,)), ((), ())), preferred_element_type=F32)


def _dot_tn(a, b):
    return lax.dot_general(a, b, (((0,), (0,)), ((), ())), preferred_element_type=F32)


def _silu(x):
    return x * (1.0 / (1.0 + jnp.exp(-x)))


def _row_loop(n_rows, rows_per, body):
    def step(i, carry):
        body(pl.multiple_of(i * rows_per, rows_per))
        return carry
    lax.fori_loop(0, n_rows // rows_per, step, 0)


def _ada_kernel(c_ref, w_ref, b_ref, o_ref):
    a = _silu(c_ref[...]).astype(BF16)
    o_ref[...] = _dot(a, w_ref[...].astype(BF16)) + b_ref[...]


def _ada(c_pad, w_ada, b_ada, tn=512):
    rows, d = c_pad.shape
    n = w_ada.shape[1]
    return pl.pallas_call(
        _ada_kernel,
        grid=(n // tn,),
        in_specs=[pl.BlockSpec((rows, d), lambda j: (0, 0)),
                  pl.BlockSpec((d, tn), lambda j: (0, j)),
                  pl.BlockSpec((1, tn), lambda j: (0, j))],
        out_specs=pl.BlockSpec((rows, tn), lambda j: (0, j)),
        out_shape=jax.ShapeDtypeStruct((rows, n), F32),
        compiler_params=_cparams(1),
        name="ada",
    )(c_pad, w_ada, b_ada.reshape(1, n))


def _row_rsqrt(x_ref, r_ref, n_rows, load=None):
    d = x_ref.shape[-1]

    def body(r0):
        for sub in range(RSQRT_GROUPS):
            rows = pl.ds(r0 + sub * SUBLANES, SUBLANES)
            acc = jnp.zeros((SUBLANES, LANES), F32)
            for j in range(d // LANES):
                cols = slice(j * LANES, (j + 1) * LANES)
                x = x_ref[rows, cols] if load is None else load(rows, cols)
                acc = acc + x * x
            r_ref[rows, :] = acc

    _row_loop(n_rows, RSQRT_GROUPS * SUBLANES, body)
    ms = jnp.sum(r_ref[...], axis=-1, keepdims=True) * (1.0 / d)
    r_ref[...] = jnp.broadcast_to(lax.rsqrt(ms + EPS), r_ref.shape)


def _scale_rows(x_ref, o_ref, r_ref, m_ref, n_rows, shift):
    d = x_ref.shape[-1]
    reps = NORM_ROWS // SUBLANES

    def body(r0):
        rows = pl.ds(r0, NORM_ROWS)
        r = r_ref[rows, :]
        for j in range(d // LANES):
            cols = slice(j * LANES, (j + 1) * LANES)
            y = x_ref[rows, cols] * r * jnp.concatenate([m_ref[0, :, cols]] * reps, axis=0)
            if shift:
                y = y + jnp.concatenate([m_ref[1, :, cols]] * reps, axis=0)
            o_ref[rows, cols] = y.astype(o_ref.dtype)

    _row_loop(n_rows, NORM_ROWS, body)


def _modulated_rms_rows(x_ref, h_ref, r_ref, m_ref, g, sc, sh, n_rows):
    d = x_ref.shape[-1]
    m_ref[0] = jnp.broadcast_to(g * (1.0 + sc), (SUBLANES, d))
    m_ref[1] = jnp.broadcast_to(sh, (SUBLANES, d))
    _row_rsqrt(x_ref, r_ref, n_rows)
    _scale_rows(x_ref, h_ref, r_ref, m_ref, n_rows, shift=True)


def _stream_weight_tiles(n_tiles, n_slots, copies, consume, before=None, after=None):
    assert n_tiles % n_slots == 0, "one loop iteration consumes one tile per slot"
    ahead = n_slots - 1

    @pl.when(pl.program_id(0) == 0)
    def _():
        for t in range(ahead):
            for cp in copies(t, t):
                cp.start()

    def ring(k, carry):
        for slot in range(n_slots):
            t = n_slots * k + slot
            for cp in copies(t, slot):
                cp.wait()
            t_next = jnp.where(t + ahead >= n_tiles, t + ahead - n_tiles, t + ahead)
            for cp in copies(t_next, (slot + ahead) % n_slots):
                cp.start()
            if before is not None:
                before(t, slot)
            consume(t, slot)
            if after is not None:
                after(t, slot)
        return carry

    lax.fori_loop(0, n_tiles // n_slots, ring, 0)

    @pl.when(pl.program_id(0) == pl.num_programs(0) - 1)
    def _():
        for t in range(ahead):
            for cp in copies(t, t):
                cp.wait()


def _cast_chunk_plan(shapes, n_grid, n_tiles):
    plan, lo = [], 0
    for rows, cols in shapes:
        assert rows % (n_grid * CAST_ROWS) == 0 and cols % CAST_COLS == 0
        rows_pg = rows // n_grid
        row_chunks = rows_pg // CAST_ROWS
        hi = lo + row_chunks * (cols // CAST_COLS)
        plan.append((lo, hi, row_chunks, rows_pg))
        lo = hi
    assert lo == n_tiles, "every (grid step, ring tile) pair converts exactly one chunk"
    return plan


def _inproj_kernel(*refs, tn, n_grid, n_cast):
    x_ref, sh_ref, sc_ref, g_ref, w_hbm, wg_ref = refs[:6]
    srcs = refs[6:6 + n_cast]
    o_ref, og_ref = refs[6 + n_cast:8 + n_cast]
    dsts = refs[8 + n_cast:8 + 2 * n_cast]
    h_ref, r_ref, m_ref, w_buf, sem = refs[8 + 2 * n_cast:13 + 2 * n_cast]
    n_tiles = o_ref.shape[1] // tn
    n_slots = w_buf.shape[0]
    i = pl.program_id(0)

    def copies(t, slot):
        return (pltpu.make_async_copy(w_hbm.at[:, pl.ds(t * tn, tn)], w_buf.at[slot], sem.at[slot]),)

    def project(t, slot):
        cols = pl.ds(pl.multiple_of(t * tn, tn), tn)
        o_ref[:, cols] = _dot(h_ref[...], w_buf[slot]).astype(o_ref.dtype)

    if not n_cast:
        _modulated_rms_rows(x_ref, h_ref, r_ref, m_ref, g_ref[...], sc_ref[0], sh_ref[0], x_ref.shape[0])
        og_ref[...] = _dot(h_ref[...], wg_ref[...])
        _stream_weight_tiles(n_tiles, n_slots, copies, project)
        return

    cin, cout, sem_in, sem_out = refs[13 + 2 * n_cast:]
    plan = _cast_chunk_plan([s.shape for s in srcs], n_grid, n_tiles)

    def for_owner(step, t, fn, extra=True):
        for a, (lo, hi, row_chunks, rows_pg) in enumerate(plan):
            @pl.when((t >= lo) & (t < hi) & extra)
            def _(a=a, lo=lo, row_chunks=row_chunks, rows_pg=rows_pg):
                j = t - lo
                row = step * rows_pg + (j % row_chunks) * CAST_ROWS
                col = (j // row_chunks) * CAST_COLS
                fn(a, (pl.ds(pl.multiple_of(row, CAST_ROWS), CAST_ROWS),
                       pl.ds(pl.multiple_of(col, CAST_COLS), CAST_COLS)))

    def copy_in(a, window, slot):
        return pltpu.make_async_copy(srcs[a].at[window], cin.at[slot], sem_in.at[slot])

    def copy_out(a, window, slot):
        return pltpu.make_async_copy(cout.at[slot], dsts[a].at[window], sem_out.at[slot])

    first_window = (pl.ds(0, CAST_ROWS), pl.ds(0, CAST_COLS))

    @pl.when(i == 0)
    def _():
        copy_in(0, first_window, 0).start()

    _modulated_rms_rows(x_ref, h_ref, r_ref, m_ref, g_ref[...], sc_ref[0], sh_ref[0], x_ref.shape[0])
    og_ref[...] = _dot(h_ref[...], wg_ref[...])

    def before(t, slot):
        copy_in(0, first_window, slot).wait()

        @pl.when(i * n_tiles + t >= n_slots)
        def _():
            copy_out(0, first_window, slot).wait()

        wrap = t + 1 == n_tiles
        for_owner(jnp.where(wrap, i + 1, i), jnp.where(wrap, 0, t + 1),
                  lambda a, window: copy_in(a, window, (slot + 1) % n_slots).start(),
                  extra=jnp.logical_not(wrap & (i == n_grid - 1)))

    def consume(t, slot):
        project(t, slot)
        cout[slot] = cin[slot].astype(BF16)

    def after(t, slot):
        for_owner(i, t, lambda a, window: copy_out(a, window, slot).start())

    _stream_weight_tiles(n_tiles, n_slots, copies, consume, before, after)

    @pl.when(i == n_grid - 1)
    def _():
        for slot in range(n_slots):
            copy_out(0, first_window, slot).wait()


def _in_proj(x2, mod3, b0, seq, norm_g, w_bf, wg_bf, cast=(), tm=512, tn=512):
    m, d = x2.shape
    n = min(w_bf.shape[1], PROJ_WIDTH)
    assert n % tn == 0
    n_grid = m // tm
    mod_spec = lambda k: pl.BlockSpec((1, 1, d), lambda i: ((b0 + (i * tm) // seq) * 6 + k, 0, 0))
    hbm = pl.BlockSpec(memory_space=pl.ANY)
    scratch = [pltpu.VMEM((tm, d), BF16), pltpu.VMEM((tm, LANES), F32), pltpu.VMEM((2, SUBLANES, d), F32),
               pltpu.VMEM((INPROJ_SLOTS, d, tn), BF16), pltpu.SemaphoreType.DMA((INPROJ_SLOTS,))]
    if cast:
        scratch += [pltpu.VMEM((INPROJ_SLOTS, CAST_ROWS, CAST_COLS), F32),
                    pltpu.VMEM((INPROJ_SLOTS, CAST_ROWS, CAST_COLS), BF16),
                    pltpu.SemaphoreType.DMA((INPROJ_SLOTS,)), pltpu.SemaphoreType.DMA((INPROJ_SLOTS,))]
    return pl.pallas_call(
        functools.partial(_inproj_kernel, tn=tn, n_grid=n_grid, n_cast=len(cast)),
        grid=(n_grid,),
        in_specs=[pl.BlockSpec((tm, d), lambda i: (i, 0)),
                  mod_spec(0), mod_spec(1),
                  pl.BlockSpec((1, d), lambda i: (0, 0)),
                  hbm,
                  pl.BlockSpec((d, GATE_PAD), lambda i: (0, 0))] + [hbm] * len(cast),
        out_specs=[pl.BlockSpec((tm, n), lambda i: (i, 0)),
                   pl.BlockSpec((tm, GATE_PAD), lambda i: (i, 0))] + [hbm] * len(cast),
        out_shape=[jax.ShapeDtypeStruct((m, n), BF16),
                   jax.ShapeDtypeStruct((m, GATE_PAD), F32)]
                  + [jax.ShapeDtypeStruct(a.shape, BF16) for a in cast],
        scratch_shapes=scratch,
        compiler_params=_cparams(1),
        name="in_proj_cast" if cast else "in_proj",
    )(x2, mod3, mod3, norm_g, w_bf, wg_bf, *cast)


def _t5_bucket_table():
    half = N_BUCKETS // 2
    max_exact = half // 2
    qi = np.arange(BLOCK)[:, None]
    kj = np.arange(3 * BLOCK)[None, :]
    rel = kj - BLOCK - qi
    n = np.abs(rel)
    nf = np.maximum(n, 1).astype(np.float32)
    large = max_exact + (np.log(nf / max_exact) / math.log(MAX_DISTANCE / max_exact)
                         * (half - max_exact)).astype(np.int32)
    large = np.minimum(large, half - 1)
    return (np.where(rel > 0, half, 0) + np.where(n < max_exact, n, large)).astype(np.int32)


def _attn_kernel(rb_ref, sink_ref, bucket_ref, q_ref, k_ref, v_ref, o_ref, tbl_ref, *, nb):
    b = pl.program_id(0)
    n = pl.program_id(1)

    @pl.when((b == 0) & (n == 0))
    def _():
        qi = lax.broadcasted_iota(jnp.int32, (BLOCK, 3 * BLOCK), 0)
        kj = lax.broadcasted_iota(jnp.int32, (BLOCK, 3 * BLOCK), 1)
        in_band = jnp.abs(kj - BLOCK - qi) <= WINDOW
        bucket = bucket_ref[...]

        def per_head(h, carry):
            def per_bucket(bk, acc):
                return jnp.where(bucket == bk, rb_ref[bk, h], acc)
            acc = lax.fori_loop(0, N_BUCKETS, per_bucket, jnp.zeros((BLOCK, 3 * BLOCK), F32)) * LOG2E
            for v in range(4):
                keep = in_band
                if v & 1:
                    keep = keep & (kj >= BLOCK)
                if v & 2:
                    keep = keep & (kj < 2 * BLOCK)
                tbl_ref[v, h] = jnp.where(keep, acc, NEG_INF)
            return carry

        lax.fori_loop(0, N_ATTN_HEADS, per_head, 0)

    r_prev = pl.multiple_of(jnp.maximum(n - 1, 0) * BLOCK, BLOCK)
    r_cur = pl.multiple_of(n * BLOCK, BLOCK)
    r_next = pl.multiple_of(jnp.minimum(n + 1, nb - 1) * BLOCK, BLOCK)
    variant = (n == 0).astype(jnp.int32) + 2 * (n == nb - 1).astype(jnp.int32)
    scale2 = HEAD_DIM ** -0.5 * LOG2E

    for g in range(N_KV_HEADS):
        kc = slice(g * HEAD_DIM, (g + 1) * HEAD_DIM)
        k3 = jnp.concatenate([k_ref[0, pl.ds(r_prev, BLOCK), kc],
                              k_ref[0, pl.ds(r_cur, BLOCK), kc],
                              k_ref[0, pl.ds(r_next, BLOCK), kc]], axis=0)
        v3 = jnp.concatenate([v_ref[0, pl.ds(r_prev, BLOCK), kc],
                              v_ref[0, pl.ds(r_cur, BLOCK), kc],
                              v_ref[0, pl.ds(r_next, BLOCK), kc]], axis=0)
        heads = [g * GQA_GROUP + j for j in range(GQA_GROUP)]
        qs = jnp.concatenate([q_ref[0, :, h * HEAD_DIM:(h + 1) * HEAD_DIM] for h in heads], axis=0)
        s = _dot_nt(qs, k3)
        ps, inv_dens = [], []
        for j, h in enumerate(heads):
            sj = s[j * BLOCK:(j + 1) * BLOCK] * scale2 + tbl_ref[variant, h]
            sink = sink_ref[0, h] * LOG2E
            m = jnp.maximum(jnp.max(sj, axis=-1, keepdims=True), sink)
            p = jnp.exp2(sj - m)
            den = jnp.sum(p, axis=-1, keepdims=True) + jnp.exp2(sink - m)
            ps.append(p.astype(BF16))
            inv_dens.append(1.0 / den)
        o = _dot(jnp.concatenate(ps, axis=0), v3)
        for j, h in enumerate(heads):
            o_ref[0, :, h * HEAD_DIM:(h + 1) * HEAD_DIM] = (
                o[j * BLOCK:(j + 1) * BLOCK] * inv_dens[j]).astype(o_ref.dtype)


def _attention(proj3, rel_bias, sink, bucket):
    bsz, seq, _ = proj3.shape
    nb = seq // BLOCK
    smem = pl.BlockSpec(memory_space=pltpu.SMEM)
    return pl.pallas_call(
        functools.partial(_attn_kernel, nb=nb),
        grid=(bsz, nb),
        in_specs=[smem, smem,
                  pl.BlockSpec((BLOCK, 3 * BLOCK), lambda b, n: (0, 0)),
                  pl.BlockSpec((1, BLOCK, ATTN_WIDTH), lambda b, n: (b, n, OFF_QA // ATTN_WIDTH)),
                  pl.BlockSpec((1, seq, KV_WIDTH), lambda b, n: (b, 0, OFF_KA // KV_WIDTH)),
                  pl.BlockSpec((1, seq, KV_WIDTH), lambda b, n: (b, 0, OFF_VA // KV_WIDTH))],
        out_specs=pl.BlockSpec((1, BLOCK, ATTN_WIDTH), lambda b, n: (b, n, 0)),
        out_shape=jax.ShapeDtypeStruct((bsz, seq, ATTN_WIDTH), BF16),
        scratch_shapes=[pltpu.VMEM((4, N_ATTN_HEADS, BLOCK, 3 * BLOCK), F32)],
        compiler_params=_cparams(2),
        name="window_attn",
    )(rel_bias, sink, bucket, proj3, proj3, proj3)


def _gla_tri_mats():
    t = np.arange(GLA_BLOCK)
    same = (t[:, None] // GLA_CHUNK) == (t[None, :] // GLA_CHUNK)
    fwd = same & (t[None, :] <= t[:, None])
    bwd = same & (t[None, :] >= t[:, None])
    return np.stack([fwd, bwd]).astype(np.float32)


def _gla_kernel(tri_ref, msk_ref, q_ref, k_ref, v_ref, r_ref, gt_ref, wa_ref, ba_ref, ng_ref, o_ref,
                bcum_ref, qt_ref, stb_ref, st_ref, u_ref, acc_ref, *, seq):
    c = GLA_CHUNK
    cpb = GLA_BLOCK // c
    nblk = seq // GLA_BLOCK
    scale = GLA_DK ** -0.5
    anchor = (c // 2 - 1, c // 2)
    last = (c - 1, 0)

    def gates(d, bi):
        rows = pl.ds(pl.multiple_of(bi * GLA_BLOCK, GLA_BLOCK), GLA_BLOCK)
        z = _dot(gt_ref[0, rows, :].astype(BF16), wa_ref[d]) + ba_ref[d]
        lg = (jnp.minimum(z, 0.0) - jnp.log(1.0 + jnp.exp(-jnp.abs(z)))) * (1.0 / GATE_TEMP)
        hi = lg.astype(BF16)
        lo = (lg - hi.astype(F32)).astype(BF16)
        cum = _dot(tri_ref[d], jnp.concatenate([hi, lo], axis=1))
        bcum_ref[d, rows, :] = cum[:, :GLA_DK] + cum[:, GLA_DK:]

    st_ref[...] = jnp.zeros_like(st_ref)
    for sub in range(GLA_UNROLL):
        gates(0, sub)
        gates(1, nblk - 1 - sub)

    def scan_blocks(i, carry):
        for sub in range(GLA_UNROLL):
            ib = i * GLA_UNROLL + sub
            for d, bi in ((0, ib), (1, nblk - 1 - ib)):
                r0 = pl.multiple_of(bi * GLA_BLOCK, GLA_BLOCK)
                lanes = slice(d * GLA_DK, (d + 1) * GLA_DK)
                decays, qps, kps = [], [], []
                for ci in range(cpb):
                    rows = pl.ds(r0 + ci * c, c)
                    bc = bcum_ref[d, rows, :]
                    b_a = bc[anchor[d]:anchor[d] + 1]
                    b_l = bc[last[d]:last[d] + 1]
                    qpf = q_ref[0, rows, :].astype(F32) * scale * jnp.exp(bc - b_a)
                    kpf = k_ref[0, rows, :].astype(F32) * jnp.exp(b_a - bc)
                    qps.append(qpf.astype(BF16))
                    kps.append(kpf.astype(BF16))
                    qt_ref[rows, lanes] = (qpf * jnp.exp(b_a)).astype(BF16)
                    kd = (kpf * jnp.exp(b_l - b_a)).astype(BF16)
                    u_ref[d, sub * cpb + ci] = _dot_tn(v_ref[0, rows, :], kd)
                    decays.append(jnp.exp(b_l))
                rows = pl.ds(r0, GLA_BLOCK)
                s = _dot_nt(jnp.concatenate(qps, axis=0), jnp.concatenate(kps, axis=0))
                a = jnp.where(msk_ref[d] > 0.5, s, 0.0).astype(BF16)
                acc_ref[d, rows, :] = _dot(a, v_ref[0, rows, :])
                for s in range(GLA_DV // GLA_SLAB):
                    sl = slice(s * GLA_SLAB, (s + 1) * GLA_SLAB)
                    st = st_ref[d, sl, :]
                    for ci in (range(cpb) if d == 0 else reversed(range(cpb))):
                        stb_ref[bi * cpb + ci, sl, lanes] = st.astype(BF16)
                        st = st * decays[ci] + u_ref[d, sub * cpb + ci, sl, :]
                    st_ref[d, sl, :] = st
        for sub in range(GLA_UNROLL):
            ib = (i + 1) * GLA_UNROLL + sub
            gates(0, jnp.minimum(ib, nblk - 1))
            gates(1, jnp.maximum(nblk - 1 - ib, 0))
        return carry

    lax.fori_loop(0, nblk // GLA_UNROLL, scan_blocks, 0)

    ng = ng_ref[...]

    def post(r0):
        for ci in range(2 * cpb):
            rows = pl.ds(r0 + ci * c, c)
            o = (acc_ref[0, rows, :] + acc_ref[1, rows, :]
                 + _dot_nt(qt_ref[rows, :], stb_ref[lax.div(r0, c) + ci]))
            ms = jnp.sum(o * o, axis=-1, keepdims=True) * (1.0 / GLA_DV)
            on = o * lax.rsqrt(ms + EPS) * ng
            o_ref[0, rows, :] = (on * _silu(r_ref[0, rows, :].astype(F32))).astype(o_ref.dtype)

    _row_loop(seq, 2 * GLA_BLOCK, post)


def _gla(proj3, gates3, tri, wa_pad, ba, norm_g):
    bsz, seq, _ = proj3.shape
    nc = seq // GLA_CHUNK
    whole = lambda shape: pl.BlockSpec(shape, lambda b, h: (0,) * len(shape))
    return pl.pallas_call(
        functools.partial(_gla_kernel, seq=seq),
        grid=(bsz, N_GLA_HEADS),
        in_specs=[whole((2, GLA_BLOCK, GLA_BLOCK)),
                  whole((2, GLA_BLOCK, GLA_BLOCK)),
                  pl.BlockSpec((1, seq, GLA_DK), lambda b, h: (b, 0, OFF_QG // GLA_DK + h)),
                  pl.BlockSpec((1, seq, GLA_DK), lambda b, h: (b, 0, OFF_KG // GLA_DK + h)),
                  pl.BlockSpec((1, seq, GLA_DV), lambda b, h: (b, 0, OFF_VG // GLA_DV + h)),
                  pl.BlockSpec((1, seq, GLA_DV), lambda b, h: (b, 0, OFF_RG // GLA_DV + h)),
                  pl.BlockSpec((1, seq, GATE_PAD), lambda b, h: (b, 0, 0)),
                  pl.BlockSpec((2, GATE_PAD, GLA_DK), lambda b, h: (0, 0, h)),
                  pl.BlockSpec((2, 1, GLA_DK), lambda b, h: (0, 0, h)),
                  whole((1, GLA_DV))],
        out_specs=pl.BlockSpec((1, seq, GLA_DV), lambda b, h: (b, 0, h)),
        out_shape=jax.ShapeDtypeStruct((bsz, seq, GLA_V_WIDTH), BF16),
        scratch_shapes=[pltpu.VMEM((2, seq, GLA_DK), F32),
                        pltpu.VMEM((seq, 2 * GLA_DK), BF16),
                        pltpu.VMEM((nc, GLA_DV, 2 * GLA_DK), BF16),
                        pltpu.VMEM((2, GLA_DV, GLA_DK), F32),
                        pltpu.VMEM((2, GLA_UNROLL * GLA_BLOCK // GLA_CHUNK, GLA_DV, GLA_DK), F32),
                        pltpu.VMEM((2, seq, GLA_DV), F32)],
        compiler_params=_cparams(2),
        name="gla",
    )(tri.astype(BF16), tri, proj3, proj3, proj3, proj3, gates3, wa_pad, ba, norm_g)


def _outproj_kernel(a_ref, g_ref, wa_ref, wg_ref, x_ref, g1_ref, o_ref):
    acc = _dot(a_ref[...], wa_ref[...]) + _dot(g_ref[...], wg_ref[...])
    o_ref[...] = x_ref[...] + g1_ref[0] * acc


def _out_proj(attn2, gla2, w_bf, x2, mod3, b0, seq, tm=1024, tn=1024):
    m, d = x2.shape
    ka = attn2.shape[1]
    kg = gla2.shape[1]
    return pl.pallas_call(
        _outproj_kernel,
        grid=(m // tm, d // tn),
        in_specs=[pl.BlockSpec((tm, ka), lambda i, j: (i, 0)),
                  pl.BlockSpec((tm, kg), lambda i, j: (i, 0)),
                  pl.BlockSpec((ka, tn), lambda i, j: (0, j)),
                  pl.BlockSpec((kg, tn), lambda i, j: (ka // kg, j)),
                  pl.BlockSpec((tm, tn), lambda i, j: (i, j)),
                  pl.BlockSpec((1, 1, tn), lambda i, j: ((b0 + (i * tm) // seq) * 6 + 2, 0, j))],
        out_specs=pl.BlockSpec((tm, tn), lambda i, j: (i, j)),
        out_shape=jax.ShapeDtypeStruct((m, d), F32),
        compiler_params=_cparams(2),
        name="out_proj",
    )(attn2, gla2, w_bf, w_bf, x2, mod3)


def _mlp_kernel(x_ref, sh_ref, sc_ref, g2_ref, ng_ref, fg_ref, w1_hbm, w2_hbm, o_ref,
                h_ref, r_ref, m_ref, w1_buf, w2_buf, sem, *, tf, tn):
    tm, d = x_ref.shape

    _modulated_rms_rows(x_ref, h_ref, r_ref, m_ref, ng_ref[...], sc_ref[0], sh_ref[0], tm)
    o_ref[...] = jnp.zeros_like(o_ref)

    def copies(f, slot):
        return (pltpu.make_async_copy(w1_hbm.at[:, pl.ds(f * tf, tf)], w1_buf.at[slot], sem.at[0, slot]),
                pltpu.make_async_copy(w2_hbm.at[pl.ds(f * tf, tf), :], w2_buf.at[slot], sem.at[1, slot]))

    def consume(f, slot):
        u = _dot(h_ref[...], w1_buf[slot])
        u = jnp.square(jnp.maximum(u, 0.0)).astype(BF16)
        for c in range(d // tn):
            cols = slice(c * tn, (c + 1) * tn)
            o_ref[:, cols] += _dot(u, w2_buf[slot, :, cols])

    _stream_weight_tiles(w1_hbm.shape[1] // tf, w1_buf.shape[0], copies, consume)

    m_ref[0] = jnp.broadcast_to(fg_ref[...], (SUBLANES, d))
    m_ref[1] = jnp.broadcast_to(g2_ref[0], (SUBLANES, d))

    def residual(rows, cols):
        x2 = x_ref[rows, cols] + m_ref[1, :, cols] * o_ref[rows, cols]
        o_ref[rows, cols] = x2
        return x2

    _row_rsqrt(o_ref, r_ref, tm, load=residual)
    _scale_rows(o_ref, o_ref, r_ref, m_ref, tm, shift=False)


def _mlp(x1, mod3, b0, seq, norm_g, final_g, w1_bf, w2_bf, tm=512, tf=512, tn=512):
    m, d = x1.shape
    dff = w1_bf.shape[1]
    mod_spec = lambda k: pl.BlockSpec((1, 1, d), lambda i: ((b0 + (i * tm) // seq) * 6 + k, 0, 0))
    return pl.pallas_call(
        functools.partial(_mlp_kernel, tf=tf, tn=tn),
        grid=(m // tm,),
        in_specs=[pl.BlockSpec((tm, d), lambda i: (i, 0)),
                  mod_spec(3), mod_spec(4), mod_spec(5),
                  pl.BlockSpec((1, d), lambda i: (0, 0)),
                  pl.BlockSpec((1, d), lambda i: (0, 0)),
                  pl.BlockSpec(memory_space=pl.ANY),
                  pl.BlockSpec(memory_space=pl.ANY)],
        out_specs=pl.BlockSpec((tm, d), lambda i: (i, 0)),
        out_shape=jax.ShapeDtypeStruct((m, d), F32),
        scratch_shapes=[pltpu.VMEM((tm, d), BF16), pltpu.VMEM((tm, LANES), F32),
                        pltpu.VMEM((2, SUBLANES, d), F32),
                        pltpu.VMEM((2, d, tf), BF16), pltpu.VMEM((2, tf, d), BF16),
                        pltpu.SemaphoreType.DMA((2, 2))],
        compiler_params=_cparams(1),
        name="mlp",
    )(x1, mod3, mod3, mod3, norm_g, final_g, w1_bf, w2_bf)


def _trunk_group(x, mod3, b0, p):
    bsz, seq, d = x.shape
    x2 = x.reshape(bsz * seq, d)
    names = p.pop("to_cast", ())
    proj, gates, *converted = _in_proj(x2, mod3, b0, seq, p["norm1_g"], p["w_in"], p["w_gate"],
                                       cast=tuple(p[k] for k in names))
    p.update(zip(names, converted))
    proj3 = proj.reshape(bsz, seq, PROJ_WIDTH)
    gates3 = gates.reshape(bsz, seq, GATE_PAD)
    attn = _attention(proj3, p["rel_bias"], p["sink"], p["bucket"])
    gla = _gla(proj3, gates3, p["tri"], p["wa_pad"], p["ba"], p["gla_norm_g"])
    x1 = _out_proj(attn.reshape(bsz * seq, ATTN_WIDTH), gla.reshape(bsz * seq, GLA_V_WIDTH),
                   p["w_out"], x2, mod3, b0, seq)
    y = _mlp(x1, mod3, b0, seq, p["norm2_g"], p["final_g"], p["w_mlp_in"], p["w_mlp_out"])
    return y.reshape(bsz, seq, d)


def kernel(x_prompt, x_sample, c_prompt, c_sample, w_ada, b_ada, norm1_g, w_in, gla_wa_fwd, gla_ba_fwd, gla_wa_bwd, gla_ba_bwd, gla_norm_g, attn_sink, rel_bias, w_out, norm2_g, w_mlp_in, w_mlp_out, final_g):
    assert w_ada.shape[0] == 1, "single-layer trunk"
    d = x_prompt.shape[-1]
    bp, bs = c_prompt.shape[0], c_sample.shape[0]

    rows = -(-(bp + bs) // 16) * 16
    c_pad = jnp.zeros((rows, d), F32).at[:bp].set(c_prompt).at[bp:bp + bs].set(c_sample)
    mod = _ada(c_pad, w_ada[0], b_ada[0])
    mod3 = mod.reshape(rows * 6, 1, d)

    w_in0 = w_in[0]
    wa_pad = jnp.zeros((2, GATE_PAD, GLA_K_WIDTH), F32)
    wa_pad = wa_pad.at[0, :GATE_RANK].set(gla_wa_fwd[0]).at[1, GATE_RANK:2 * GATE_RANK].set(gla_wa_bwd[0])
    p = {
        "norm1_g": norm1_g[0].reshape(1, d),
        "norm2_g": norm2_g[0].reshape(1, d),
        "final_g": final_g.reshape(1, d),
        "w_in": w_in0.astype(BF16),
        "w_gate": jnp.pad(w_in0[:, OFF_GATE:], ((0, 0), (0, GATE_PAD - 2 * GATE_RANK))).astype(BF16),
        "w_out": w_out[0],
        "w_mlp_in": w_mlp_in[0],
        "w_mlp_out": w_mlp_out[0],
        "to_cast": ("w_mlp_in", "w_mlp_out", "w_out"),
        "wa_pad": wa_pad.astype(BF16),
        "ba": jnp.stack([gla_ba_fwd[0], gla_ba_bwd[0]]).reshape(2, 1, GLA_K_WIDTH),
        "gla_norm_g": gla_norm_g[0].reshape(1, GLA_DV),
        "rel_bias": rel_bias,
        "sink": attn_sink[0].reshape(1, N_ATTN_HEADS),
        "bucket": jnp.asarray(_t5_bucket_table()),
        "tri": jnp.asarray(_gla_tri_mats()),
    }
    y_prompt = _trunk_group(x_prompt, mod3, 0, p)
    y_sample = _trunk_group(x_sample, mod3, bp, p)
    return (y_prompt, y_sample)
```

```python
import functools
import math

import numpy as np
import jax
import jax.numpy as jnp
from jax import lax
from jax.experimental import pallas as pl
from jax.experimental.pallas import tpu as pltpu

F32 = jnp.float32
BF16 = jnp.bfloat16

HEAD_DIM = 128
N_ATTN_HEADS = 16
N_KV_HEADS = 4
GQA_GROUP = N_ATTN_HEADS // N_KV_HEADS
ATTN_WIDTH = N_ATTN_HEADS * HEAD_DIM
KV_WIDTH = N_KV_HEADS * HEAD_DIM
WINDOW = 128
BLOCK = 128
GLA_DV = 256
GLA_DK = 128
N_GLA_HEADS = 8
GLA_K_WIDTH = N_GLA_HEADS * GLA_DK
GLA_V_WIDTH = N_GLA_HEADS * GLA_DV
GATE_RANK = 16
GATE_TEMP = 16.0
N_BUCKETS = 32
MAX_DISTANCE = 128
EPS = 1e-6
NEG_INF = -1e30
LOG2E = math.log2(math.e)

OFF_QA = 0
OFF_KA = OFF_QA + ATTN_WIDTH
OFF_VA = OFF_KA + KV_WIDTH
OFF_QG = OFF_VA + KV_WIDTH
OFF_KG = OFF_QG + GLA_K_WIDTH
OFF_VG = OFF_KG + GLA_K_WIDTH
OFF_RG = OFF_VG + GLA_V_WIDTH
OFF_GATE = OFF_RG + GLA_V_WIDTH
PROJ_WIDTH = OFF_GATE
GATE_PAD = 128

GLA_CHUNK = 64
GLA_BLOCK = 256
GLA_SLAB = 64
GLA_UNROLL = 2
INPROJ_SLOTS = 3
INPROJ_HALVES = 2
CAST_ROWS, CAST_COLS = 128, 4096
SUBLANES = 8
LANES = 128
NORM_ROWS = 16
RSQRT_GROUPS = 8

VMEM_LIMIT = 60 * 1024 * 1024


def _cparams(n_axes, vmem=VMEM_LIMIT):
    return pltpu.CompilerParams(dimension_semantics=("arbitrary",) * n_axes,
                                vmem_limit_bytes=vmem)


def _dot(a, b):
    return jnp.dot(a, b, preferred_element_type=F32)


def _dot_nt(a, b):
    return lax.dot_general(a, b, (((1,), (1,)), ((), ())), preferred_element_type=F32)


def _dot_tn(a, b):
    return lax.dot_general(a, b, (((0,), (0,)), ((), ())), preferred_element_type=F32)


def _silu(x):
    return x * (1.0 / (1.0 + jnp.exp(-x)))


def _row_loop(n_rows, rows_per, body):
    def step(i, carry):
        body(pl.multiple_of(i * rows_per, rows_per))
        return carry
    lax.fori_loop(0, n_rows // rows_per, step, 0)


def _ada_kernel(c_ref, w_ref, b_ref, o_ref):
    a = _silu(c_ref[...]).astype(BF16)
    o_ref[...] = _dot(a, w_ref[...].astype(BF16)) + b_ref[...]


def _ada(c_pad, w_ada, b_ada, tn=512):
    rows, d = c_pad.shape
    n = w_ada.shape[1]
    return pl.pallas_call(
        _ada_kernel,
        grid=(n // tn,),
        in_specs=[pl.BlockSpec((rows, d), lambda j: (0, 0)),
                  pl.BlockSpec((d, tn), lambda j: (0, j)),
                  pl.BlockSpec((1, tn), lambda j: (0, j))],
        out_specs=pl.BlockSpec((rows, tn), lambda j: (0, j)),
        out_shape=jax.ShapeDtypeStruct((rows, n), F32),
        compiler_params=_cparams(1),
        name="ada",
    )(c_pad, w_ada, b_ada.reshape(1, n))


def _row_rsqrt(x_ref, r_ref, n_rows, load=None):
    d = x_ref.shape[-1]

    def body(r0):
        for sub in range(RSQRT_GROUPS):
            rows = pl.ds(r0 + sub * SUBLANES, SUBLANES)
            acc = jnp.zeros((SUBLANES, LANES), F32)
            for j in range(d // LANES):
                cols = slice(j * LANES, (j + 1) * LANES)
                x = x_ref[rows, cols] if load is None else load(rows, cols)
                acc = acc + x * x
            r_ref[rows, :] = acc

    _row_loop(n_rows, RSQRT_GROUPS * SUBLANES, body)
    ms = jnp.sum(r_ref[...], axis=-1, keepdims=True) * (1.0 / d)
    r_ref[...] = jnp.broadcast_to(lax.rsqrt(ms + EPS), r_ref.shape)


def _scale_rows(x_ref, o_ref, r_ref, m_ref, n_rows, shift):
    d = x_ref.shape[-1]
    reps = NORM_ROWS // SUBLANES

    def body(r0):
        rows = pl.ds(r0, NORM_ROWS)
        r = r_ref[rows, :]
        for j in range(d // LANES):
            cols = slice(j * LANES, (j + 1) * LANES)
            y = x_ref[rows, cols] * r * jnp.concatenate([m_ref[0, :, cols]] * reps, axis=0)
            if shift:
                y = y + jnp.concatenate([m_ref[1, :, cols]] * reps, axis=0)
            o_ref[rows, cols] = y.astype(o_ref.dtype)

    _row_loop(n_rows, NORM_ROWS, body)


def _modulated_rms_rows(x_ref, h_ref, r_ref, m_ref, g, sc, sh, n_rows):
    d = x_ref.shape[-1]
    m_ref[0] = jnp.broadcast_to(g * (1.0 + sc), (SUBLANES, d))
    m_ref[1] = jnp.broadcast_to(sh, (SUBLANES, d))
    _row_rsqrt(x_ref, r_ref, n_rows)
    _scale_rows(x_ref, h_ref, r_ref, m_ref, n_rows, shift=True)


def _stream_weight_tiles(n_tiles, n_slots, copies, consume, before=None, after=None):
    assert n_tiles % n_slots == 0, "one loop iteration consumes one tile per slot"
    ahead = n_slots - 1

    @pl.when(pl.program_id(0) == 0)
    def _():
        for t in range(ahead):
            for cp in copies(t, t):
                cp.start()

    def ring(k, carry):
        for slot in range(n_slots):
            t = n_slots * k + slot
            for cp in copies(t, slot):
                cp.wait()
            t_next = jnp.where(t + ahead >= n_tiles, t + ahead - n_tiles, t + ahead)
            for cp in copies(t_next, (slot + ahead) % n_slots):
                cp.start()
            if before is not None:
                before(t, slot)
            consume(t, slot)
            if after is not None:
                after(t, slot)
        return carry

    lax.fori_loop(0, n_tiles // n_slots, ring, 0)

    @pl.when(pl.program_id(0) == pl.num_programs(0) - 1)
    def _():
        for t in range(ahead):
            for cp in copies(t, t):
                cp.wait()


def _cast_chunk_plan(shapes, n_grid, n_tiles):
    plan, lo = [], 0
    for rows, cols in shapes:
        assert rows % (n_grid * CAST_ROWS) == 0 and cols % CAST_COLS == 0
        rows_pg = rows // n_grid
        row_chunks = rows_pg // CAST_ROWS
        hi = lo + row_chunks * (cols // CAST_COLS)
        plan.append((lo, hi, row_chunks, rows_pg))
        lo = hi
    assert lo == n_tiles, "every (grid step, ring tile) pair converts exactly one chunk"
    return plan


def _inproj_kernel(*refs, tn, n_grid, n_cast):
    sh_ref, sc_ref, g_ref, wg_ref, x_hbm, w_hbm = refs[:6]
    srcs = refs[6:6 + n_cast]
    og_ref, o_hbm = refs[6 + n_cast:8 + n_cast]
    dsts = refs[8 + n_cast:8 + 2 * n_cast]
    x_buf, h_ref, r_ref, m_ref, w_buf, o_buf, sem_x, sem_w, sem_o = refs[8 + 2 * n_cast:17 + 2 * n_cast]
    n_half, half, _ = x_buf.shape
    n_tiles = o_hbm.shape[1] // tn
    n_slots = w_buf.shape[0]
    i = pl.program_id(0)

    def x_copy(step, hf):
        rows = pl.ds(pl.multiple_of((step * n_half + hf) * half, half), half)
        return pltpu.make_async_copy(x_hbm.at[rows, :], x_buf.at[hf], sem_x.at[hf])

    def w_copies(t, slot):
        return (pltpu.make_async_copy(w_hbm.at[:, pl.ds(t * tn, tn)], w_buf.at[slot], sem_w.at[slot]),)

    def o_copy(step, t, hf, slot):
        rows = pl.ds(pl.multiple_of((step * n_half + hf) * half, half), half)
        cols = pl.ds(pl.multiple_of(t * tn, tn), tn)
        return pltpu.make_async_copy(o_buf.at[slot, hf], o_hbm.at[rows, cols], sem_o.at[slot, hf])

    if n_cast:
        cin, cout, sem_in, sem_out = refs[17 + 2 * n_cast:]
        plan = _cast_chunk_plan([s.shape for s in srcs], n_grid, n_tiles)
        first_window = (pl.ds(0, CAST_ROWS), pl.ds(0, CAST_COLS))

        def for_owner(step, t, fn, extra=True):
            for a, (lo, hi, row_chunks, rows_pg) in enumerate(plan):
                @pl.when((t >= lo) & (t < hi) & extra)
                def _(a=a, lo=lo, row_chunks=row_chunks, rows_pg=rows_pg):
                    j = t - lo
                    row = step * rows_pg + (j % row_chunks) * CAST_ROWS
                    col = (j // row_chunks) * CAST_COLS
                    fn(a, (pl.ds(pl.multiple_of(row, CAST_ROWS), CAST_ROWS),
                           pl.ds(pl.multiple_of(col, CAST_COLS), CAST_COLS)))

        def copy_in(a, window, slot):
            return pltpu.make_async_copy(srcs[a].at[window], cin.at[slot], sem_in.at[slot])

        def copy_out(a, window, slot):
            return pltpu.make_async_copy(cout.at[slot], dsts[a].at[window], sem_out.at[slot])

    @pl.when(i == 0)
    def _():
        for hf in range(n_half):
            x_copy(0, hf).start()
        if n_cast:
            copy_in(0, first_window, 0).start()

    for hf in range(n_half):
        x_copy(i, hf).wait()
        _modulated_rms_rows(x_buf.at[hf], h_ref.at[pl.ds(hf * half, half)], r_ref, m_ref,
                            g_ref[...], sc_ref[0], sh_ref[0], half)

        @pl.when(i + 1 < n_grid)
        def _(hf=hf):
            x_copy(i + 1, hf).start()

    og_ref[...] = _dot(h_ref[...], wg_ref[...])

    def before(t, slot):
        @pl.when(i * n_tiles + t >= n_slots)
        def _():
            for hf in range(n_half):
                o_copy(0, 0, hf, slot).wait()
            if n_cast:
                copy_out(0, first_window, slot).wait()

        if n_cast:
            copy_in(0, first_window, slot).wait()
            wrap = t + 1 == n_tiles
            for_owner(jnp.where(wrap, i + 1, i), jnp.where(wrap, 0, t + 1),
                      lambda a, window: copy_in(a, window, (slot + 1) % n_slots).start(),
                      extra=jnp.logical_not(wrap & (i == n_grid - 1)))

    def consume(t, slot):
        for hf in range(n_half):
            o_buf[slot, hf] = _dot(h_ref[hf * half:(hf + 1) * half, :], w_buf[slot]).astype(o_buf.dtype)
        if n_cast:
            cout[slot] = cin[slot].astype(BF16)

    def after(t, slot):
        for hf in range(n_half):
            o_copy(i, t, hf, slot).start()
        if n_cast:
            for_owner(i, t, lambda a, window: copy_out(a, window, slot).start())

    _stream_weight_tiles(n_tiles, n_slots, w_copies, consume, before, after)

    @pl.when(i == n_grid - 1)
    def _():
        for slot in range(n_slots):
            for hf in range(n_half):
                o_copy(0, 0, hf, slot).wait()
            if n_cast:
                copy_out(0, first_window, slot).wait()


def _in_proj(x2, mod3, b0, seq, norm_g, w_bf, wg_bf, cast=(), half=512, tn=512):
    m, d = x2.shape
    n = min(w_bf.shape[1], PROJ_WIDTH)
    tm = INPROJ_HALVES * half
    assert n % tn == 0 and m % tm == 0 and seq % tm == 0
    n_grid = m // tm
    mod_spec = lambda k: pl.BlockSpec((1, 1, d), lambda i: ((b0 + (i * tm) // seq) * 6 + k, 0, 0))
    hbm = pl.BlockSpec(memory_space=pl.ANY)
    scratch = [pltpu.VMEM((INPROJ_HALVES, half, d), F32), pltpu.VMEM((tm, d), BF16),
               pltpu.VMEM((half, LANES), F32), pltpu.VMEM((2, SUBLANES, d), F32),
               pltpu.VMEM((INPROJ_SLOTS, d, tn), BF16), pltpu.VMEM((INPROJ_SLOTS, INPROJ_HALVES, half, tn), BF16),
               pltpu.SemaphoreType.DMA((INPROJ_HALVES,)), pltpu.SemaphoreType.DMA((INPROJ_SLOTS,)),
               pltpu.SemaphoreType.DMA((INPROJ_SLOTS, INPROJ_HALVES))]
    if cast:
        scratch += [pltpu.VMEM((INPROJ_SLOTS, CAST_ROWS, CAST_COLS), F32),
                    pltpu.VMEM((INPROJ_SLOTS, CAST_ROWS, CAST_COLS), BF16),
                    pltpu.SemaphoreType.DMA((INPROJ_SLOTS,)), pltpu.SemaphoreType.DMA((INPROJ_SLOTS,))]
    gates, proj, *converted = pl.pallas_call(
        functools.partial(_inproj_kernel, tn=tn, n_grid=n_grid, n_cast=len(cast)),
        grid=(n_grid,),
        in_specs=[mod_spec(0), mod_spec(1),
                  pl.BlockSpec((1, d), lambda i: (0, 0)),
                  pl.BlockSpec((d, GATE_PAD), lambda i: (0, 0)),
                  hbm, hbm] + [hbm] * len(cast),
        out_specs=[pl.BlockSpec((tm, GATE_PAD), lambda i: (i, 0)), hbm] + [hbm] * len(cast),
        out_shape=[jax.ShapeDtypeStruct((m, GATE_PAD), F32),
                   jax.ShapeDtypeStruct((m, n), BF16)]
                  + [jax.ShapeDtypeStruct(a.shape, BF16) for a in cast],
        scratch_shapes=scratch,
        compiler_params=_cparams(1),
        name="in_proj_cast" if cast else "in_proj",
    )(mod3, mod3, norm_g, wg_bf, x2, w_bf, *cast)
    return (proj, gates, *converted)


def _t5_bucket_table():
    half = N_BUCKETS // 2
    max_exact = half // 2
    qi = np.arange(BLOCK)[:, None]
    kj = np.arange(3 * BLOCK)[None, :]
    rel = kj - BLOCK - qi
    n = np.abs(rel)
    nf = np.maximum(n, 1).astype(np.float32)
    large = max_exact + (np.log(nf / max_exact) / math.log(MAX_DISTANCE / max_exact)
                         * (half - max_exact)).astype(np.int32)
    large = np.minimum(large, half - 1)
    return (np.where(rel > 0, half, 0) + np.where(n < max_exact, n, large)).astype(np.int32)


def _attn_kernel(rb_ref, sink_ref, bucket_ref, q_ref, k_ref, v_ref, o_ref, tbl_ref, *, nb):
    b = pl.program_id(0)
    n = pl.program_id(1)

    @pl.when((b == 0) & (n == 0))
    def _():
        qi = lax.broadcasted_iota(jnp.int32, (BLOCK, 3 * BLOCK), 0)
        kj = lax.broadcasted_iota(jnp.int32, (BLOCK, 3 * BLOCK), 1)
        in_band = jnp.abs(kj - BLOCK - qi) <= WINDOW
        bucket = bucket_ref[...]

        def per_head(h, carry):
            def per_bucket(bk, acc):
                return jnp.where(bucket == bk, rb_ref[bk, h], acc)
            acc = lax.fori_loop(0, N_BUCKETS, per_bucket, jnp.zeros((BLOCK, 3 * BLOCK), F32)) * LOG2E
            for v in range(4):
                keep = in_band
                if v & 1:
                    keep = keep & (kj >= BLOCK)
                if v & 2:
                    keep = keep & (kj < 2 * BLOCK)
                tbl_ref[v, h] = jnp.where(keep, acc, NEG_INF)
            return carry

        lax.fori_loop(0, N_ATTN_HEADS, per_head, 0)

    r_prev = pl.multiple_of(jnp.maximum(n - 1, 0) * BLOCK, BLOCK)
    r_cur = pl.multiple_of(n * BLOCK, BLOCK)
    r_next = pl.multiple_of(jnp.minimum(n + 1, nb - 1) * BLOCK, BLOCK)
    variant = (n == 0).astype(jnp.int32) + 2 * (n == nb - 1).astype(jnp.int32)
    scale2 = HEAD_DIM ** -0.5 * LOG2E

    for g in range(N_KV_HEADS):
        kc = slice(g * HEAD_DIM, (g + 1) * HEAD_DIM)
        k3 = jnp.concatenate([k_ref[0, pl.ds(r_prev, BLOCK), kc],
                              k_ref[0, pl.ds(r_cur, BLOCK), kc],
                              k_ref[0, pl.ds(r_next, BLOCK), kc]], axis=0)
        v3 = jnp.concatenate([v_ref[0, pl.ds(r_prev, BLOCK), kc],
                              v_ref[0, pl.ds(r_cur, BLOCK), kc],
                              v_ref[0, pl.ds(r_next, BLOCK), kc]], axis=0)
        heads = [g * GQA_GROUP + j for j in range(GQA_GROUP)]
        qs = jnp.concatenate([q_ref[0, :, h * HEAD_DIM:(h + 1) * HEAD_DIM] for h in heads], axis=0)
        s = _dot_nt(qs, k3)
        ps, inv_dens = [], []
        for j, h in enumerate(heads):
            sj = s[j * BLOCK:(j + 1) * BLOCK] * scale2 + tbl_ref[variant, h]
            sink = sink_ref[0, h] * LOG2E
            m = jnp.maximum(jnp.max(sj, axis=-1, keepdims=True), sink)
            p = jnp.exp2(sj - m)
            den = jnp.sum(p, axis=-1, keepdims=True) + jnp.exp2(sink - m)
            ps.append(p.astype(BF16))
            inv_dens.append(1.0 / den)
        o = _dot(jnp.concatenate(ps, axis=0), v3)
        for j, h in enumerate(heads):
            o_ref[0, :, h * HEAD_DIM:(h + 1) * HEAD_DIM] = (
                o[j * BLOCK:(j + 1) * BLOCK] * inv_dens[j]).astype(o_ref.dtype)


def _attention(proj3, rel_bias, sink, bucket):
    bsz, seq, _ = proj3.shape
    nb = seq // BLOCK
    smem = pl.BlockSpec(memory_space=pltpu.SMEM)
    return pl.pallas_call(
        functools.partial(_attn_kernel, nb=nb),
        grid=(bsz, nb),
        in_specs=[smem, smem,
                  pl.BlockSpec((BLOCK, 3 * BLOCK), lambda b, n: (0, 0)),
                  pl.BlockSpec((1, BLOCK, ATTN_WIDTH), lambda b, n: (b, n, OFF_QA // ATTN_WIDTH)),
                  pl.BlockSpec((1, seq, KV_WIDTH), lambda b, n: (b, 0, OFF_KA // KV_WIDTH)),
                  pl.BlockSpec((1, seq, KV_WIDTH), lambda b, n: (b, 0, OFF_VA // KV_WIDTH))],
        out_specs=pl.BlockSpec((1, BLOCK, ATTN_WIDTH), lambda b, n: (b, n, 0)),
        out_shape=jax.ShapeDtypeStruct((bsz, seq, ATTN_WIDTH), BF16),
        scratch_shapes=[pltpu.VMEM((4, N_ATTN_HEADS, BLOCK, 3 * BLOCK), F32)],
        compiler_params=_cparams(2),
        name="window_attn",
    )(rel_bias, sink, bucket, proj3, proj3, proj3)


def _gla_tri_mats():
    t = np.arange(GLA_BLOCK)
    same = (t[:, None] // GLA_CHUNK) == (t[None, :] // GLA_CHUNK)
    fwd = same & (t[None, :] <= t[:, None])
    bwd = same & (t[None, :] >= t[:, None])
    return np.stack([fwd, bwd]).astype(np.float32)


def _gla_kernel(tri_ref, msk_ref, q_ref, k_ref, v_ref, r_ref, gt_ref, wa_ref, ba_ref, ng_ref, o_ref,
                bcum_ref, qt_ref, stb_ref, st_ref, u_ref, acc_ref, *, seq):
    c = GLA_CHUNK
    cpb = GLA_BLOCK // c
    nblk = seq // GLA_BLOCK
    scale = GLA_DK ** -0.5
    anchor = (c // 2 - 1, c // 2)
    last = (c - 1, 0)

    def gates(d, bi):
        rows = pl.ds(pl.multiple_of(bi * GLA_BLOCK, GLA_BLOCK), GLA_BLOCK)
        z = _dot(gt_ref[0, rows, :].astype(BF16), wa_ref[d]) + ba_ref[d]
        lg = (jnp.minimum(z, 0.0) - jnp.log(1.0 + jnp.exp(-jnp.abs(z)))) * (1.0 / GATE_TEMP)
        hi = lg.astype(BF16)
        lo = (lg - hi.astype(F32)).astype(BF16)
        cum = _dot(tri_ref[d], jnp.concatenate([hi, lo], axis=1))
        bcum_ref[d, rows, :] = cum[:, :GLA_DK] + cum[:, GLA_DK:]

    st_ref[...] = jnp.zeros_like(st_ref)
    for sub in range(GLA_UNROLL):
        gates(0, sub)
        gates(1, nblk - 1 - sub)

    def scan_blocks(i, carry):
        for sub in range(GLA_UNROLL):
            ib = i * GLA_UNROLL + sub
            for d, bi in ((0, ib), (1, nblk - 1 - ib)):
                r0 = pl.multiple_of(bi * GLA_BLOCK, GLA_BLOCK)
                lanes = slice(d * GLA_DK, (d + 1) * GLA_DK)
                decays, qps, kps = [], [], []
                for ci in range(cpb):
                    rows = pl.ds(r0 + ci * c, c)
                    bc = bcum_ref[d, rows, :]
                    b_a = bc[anchor[d]:anchor[d] + 1]
                    b_l = bc[last[d]:last[d] + 1]
                    qpf = q_ref[0, rows, :].astype(F32) * scale * jnp.exp(bc - b_a)
                    kpf = k_ref[0, rows, :].astype(F32) * jnp.exp(b_a - bc)
                    qps.append(qpf.astype(BF16))
                    kps.append(kpf.astype(BF16))
                    qt_ref[rows, lanes] = (qpf * jnp.exp(b_a)).astype(BF16)
                    kd = (kpf * jnp.exp(b_l - b_a)).astype(BF16)
                    u_ref[d, sub * cpb + ci] = _dot_tn(v_ref[0, rows, :], kd)
                    decays.append(jnp.exp(b_l))
                rows = pl.ds(r0, GLA_BLOCK)
                s = _dot_nt(jnp.concatenate(qps, axis=0), jnp.concatenate(kps, axis=0))
                a = jnp.where(msk_ref[d] > 0.5, s, 0.0).astype(BF16)
                acc_ref[d, rows, :] = _dot(a, v_ref[0, rows, :])
                for s in range(GLA_DV // GLA_SLAB):
                    sl = slice(s * GLA_SLAB, (s + 1) * GLA_SLAB)
                    st = st_ref[d, sl, :]
                    for ci in (range(cpb) if d == 0 else reversed(range(cpb))):
                        stb_ref[bi * cpb + ci, sl, lanes] = st.astype(BF16)
                        st = st * decays[ci] + u_ref[d, sub * cpb + ci, sl, :]
                    st_ref[d, sl, :] = st
        for sub in range(GLA_UNROLL):
            ib = (i + 1) * GLA_UNROLL + sub
            gates(0, jnp.minimum(ib, nblk - 1))
            gates(1, jnp.maximum(nblk - 1 - ib, 0))
        return carry

    lax.fori_loop(0, nblk // GLA_UNROLL, scan_blocks, 0)

    ng = ng_ref[...]

    def post(r0):
        for ci in range(2 * cpb):
            rows = pl.ds(r0 + ci * c, c)
            o = (acc_ref[0, rows, :] + acc_ref[1, rows, :]
                 + _dot_nt(qt_ref[rows, :], stb_ref[lax.div(r0, c) + ci]))
            ms = jnp.sum(o * o, axis=-1, keepdims=True) * (1.0 / GLA_DV)
            on = o * lax.rsqrt(ms + EPS) * ng
            o_ref[0, rows, :] = (on * _silu(r_ref[0, rows, :].astype(F32))).astype(o_ref.dtype)

    _row_loop(seq, 2 * GLA_BLOCK, post)


def _gla(proj3, gates3, tri, wa_pad, ba, norm_g):
    bsz, seq, _ = proj3.shape
    nc = seq // GLA_CHUNK
    whole = lambda shape: pl.BlockSpec(shape, lambda b, h: (0,) * len(shape))
    return pl.pallas_call(
        functools.partial(_gla_kernel, seq=seq),
        grid=(bsz, N_GLA_HEADS),
        in_specs=[whole((2, GLA_BLOCK, GLA_BLOCK)),
                  whole((2, GLA_BLOCK, GLA_BLOCK)),
                  pl.BlockSpec((1, seq, GLA_DK), lambda b, h: (b, 0, OFF_QG // GLA_DK + h)),
                  pl.BlockSpec((1, seq, GLA_DK), lambda b, h: (b, 0, OFF_KG // GLA_DK + h)),
                  pl.BlockSpec((1, seq, GLA_DV), lambda b, h: (b, 0, OFF_VG // GLA_DV + h)),
                  pl.BlockSpec((1, seq, GLA_DV), lambda b, h: (b, 0, OFF_RG // GLA_DV + h)),
                  pl.BlockSpec((1, seq, GATE_PAD), lambda b, h: (b, 0, 0)),
                  pl.BlockSpec((2, GATE_PAD, GLA_DK), lambda b, h: (0, 0, h)),
                  pl.BlockSpec((2, 1, GLA_DK), lambda b, h: (0, 0, h)),
                  whole((1, GLA_DV))],
        out_specs=pl.BlockSpec((1, seq, GLA_DV), lambda b, h: (b, 0, h)),
        out_shape=jax.ShapeDtypeStruct((bsz, seq, GLA_V_WIDTH), BF16),
        scratch_shapes=[pltpu.VMEM((2, seq, GLA_DK), F32),
                        pltpu.VMEM((seq, 2 * GLA_DK), BF16),
                        pltpu.VMEM((nc, GLA_DV, 2 * GLA_DK), BF16),
                        pltpu.VMEM((2, GLA_DV, GLA_DK), F32),
                        pltpu.VMEM((2, GLA_UNROLL * GLA_BLOCK // GLA_CHUNK, GLA_DV, GLA_DK), F32),
                        pltpu.VMEM((2, seq, GLA_DV), F32)],
        compiler_params=_cparams(2),
        name="gla",
    )(tri.astype(BF16), tri, proj3, proj3, proj3, proj3, gates3, wa_pad, ba, norm_g)


def _outproj_kernel(a_ref, g_ref, wa_ref, wg_ref, x_ref, g1_ref, o_ref):
    acc = _dot(a_ref[...], wa_ref[...]) + _dot(g_ref[...], wg_ref[...])
    o_ref[...] = x_ref[...] + g1_ref[0] * acc


def _out_proj(attn2, gla2, w_bf, x2, mod3, b0, seq, tm=1024, tn=1024):
    m, d = x2.shape
    ka = attn2.shape[1]
    kg = gla2.shape[1]
    return pl.pallas_call(
        _outproj_kernel,
        grid=(m // tm, d // tn),
        in_specs=[pl.BlockSpec((tm, ka), lambda i, j: (i, 0)),
                  pl.BlockSpec((tm, kg), lambda i, j: (i, 0)),
                  pl.BlockSpec((ka, tn), lambda i, j: (0, j)),
                  pl.BlockSpec((kg, tn), lambda i, j: (ka // kg, j)),
                  pl.BlockSpec((tm, tn), lambda i, j: (i, j)),
                  pl.BlockSpec((1, 1, tn), lambda i, j: ((b0 + (i * tm) // seq) * 6 + 2, 0, j))],
        out_specs=pl.BlockSpec((tm, tn), lambda i, j: (i, j)),
        out_shape=jax.ShapeDtypeStruct((m, d), F32),
        compiler_params=_cparams(2),
        name="out_proj",
    )(attn2, gla2, w_bf, w_bf, x2, mod3)


def _mlp_kernel(x_ref, sh_ref, sc_ref, g2_ref, ng_ref, fg_ref, w1_hbm, w2_hbm, o_ref,
                h_ref, r_ref, m_ref, w1_buf, w2_buf, sem, *, tf, tn):
    tm, d = x_ref.shape

    _modulated_rms_rows(x_ref, h_ref, r_ref, m_ref, ng_ref[...], sc_ref[0], sh_ref[0], tm)
    o_ref[...] = jnp.zeros_like(o_ref)

    def copies(f, slot):
        return (pltpu.make_async_copy(w1_hbm.at[:, pl.ds(f * tf, tf)], w1_buf.at[slot], sem.at[0, slot]),
                pltpu.make_async_copy(w2_hbm.at[pl.ds(f * tf, tf), :], w2_buf.at[slot], sem.at[1, slot]))

    def consume(f, slot):
        u = _dot(h_ref[...], w1_buf[slot])
        u = jnp.square(jnp.maximum(u, 0.0)).astype(BF16)
        for c in range(d // tn):
            cols = slice(c * tn, (c + 1) * tn)
            o_ref[:, cols] += _dot(u, w2_buf[slot, :, cols])

    _stream_weight_tiles(w1_hbm.shape[1] // tf, w1_buf.shape[0], copies, consume)

    m_ref[0] = jnp.broadcast_to(fg_ref[...], (SUBLANES, d))
    m_ref[1] = jnp.broadcast_to(g2_ref[0], (SUBLANES, d))

    def residual(rows, cols):
        x2 = x_ref[rows, cols] + m_ref[1, :, cols] * o_ref[rows, cols]
        o_ref[rows, cols] = x2
        return x2

    _row_rsqrt(o_ref, r_ref, tm, load=residual)
    _scale_rows(o_ref, o_ref, r_ref, m_ref, tm, shift=False)


def _mlp(x1, mod3, b0, seq, norm_g, final_g, w1_bf, w2_bf, tm=512, tf=512, tn=512):
    m, d = x1.shape
    dff = w1_bf.shape[1]
    mod_spec = lambda k: pl.BlockSpec((1, 1, d), lambda i: ((b0 + (i * tm) // seq) * 6 + k, 0, 0))
    return pl.pallas_call(
        functools.partial(_mlp_kernel, tf=tf, tn=tn),
        grid=(m // tm,),
        in_specs=[pl.BlockSpec((tm, d), lambda i: (i, 0)),
                  mod_spec(3), mod_spec(4), mod_spec(5),
                  pl.BlockSpec((1, d), lambda i: (0, 0)),
                  pl.BlockSpec((1, d), lambda i: (0, 0)),
                  pl.BlockSpec(memory_space=pl.ANY),
                  pl.BlockSpec(memory_space=pl.ANY)],
        out_specs=pl.BlockSpec((tm, d), lambda i: (i, 0)),
        out_shape=jax.ShapeDtypeStruct((m, d), F32),
        scratch_shapes=[pltpu.VMEM((tm, d), BF16), pltpu.VMEM((tm, LANES), F32),
                        pltpu.VMEM((2, SUBLANES, d), F32),
                        pltpu.VMEM((2, d, tf), BF16), pltpu.VMEM((2, tf, d), BF16),
                        pltpu.SemaphoreType.DMA((2, 2))],
        compiler_params=_cparams(1),
        name="mlp",
    )(x1, mod3, mod3, mod3, norm_g, final_g, w1_bf, w2_bf)


def _trunk_group(x, mod3, b0, p):
    bsz, seq, d = x.shape
    x2 = x.reshape(bsz * seq, d)
    names = p.pop("to_cast", ())
    proj, gates, *converted = _in_proj(x2, mod3, b0, seq, p["norm1_g"], p["w_in"], p["w_gate"],
                                       cast=tuple(p[k] for k in names))
    p.update(zip(names, converted))
    proj3 = proj.reshape(bsz, seq, PROJ_WIDTH)
    gates3 = gates.reshape(bsz, seq, GATE_PAD)
    attn = _attention(proj3, p["rel_bias"], p["sink"], p["bucket"])
    gla = _gla(proj3, gates3, p["tri"], p["wa_pad"], p["ba"], p["gla_norm_g"])
    x1 = _out_proj(attn.reshape(bsz * seq, ATTN_WIDTH), gla.reshape(bsz * seq, GLA_V_WIDTH),
                   p["w_out"], x2, mod3, b0, seq)
    y = _mlp(x1, mod3, b0, seq, p["norm2_g"], p["final_g"], p["w_mlp_in"], p["w_mlp_out"])
    return y.reshape(bsz, seq, d)


def kernel(x_prompt, x_sample, c_prompt, c_sample, w_ada, b_ada, norm1_g, w_in, gla_wa_fwd, gla_ba_fwd, gla_wa_bwd, gla_ba_bwd, gla_norm_g, attn_sink, rel_bias, w_out, norm2_g, w_mlp_in, w_mlp_out, final_g):
    assert w_ada.shape[0] == 1, "single-layer trunk"
    d = x_prompt.shape[-1]
    bp, bs = c_prompt.shape[0], c_sample.shape[0]

    rows = -(-(bp + bs) // 16) * 16
    c_pad = jnp.zeros((rows, d), F32).at[:bp].set(c_prompt).at[bp:bp + bs].set(c_sample)
    mod = _ada(c_pad, w_ada[0], b_ada[0])
    mod3 = mod.reshape(rows * 6, 1, d)

    w_in0 = w_in[0]
    wa_pad = jnp.zeros((2, GATE_PAD, GLA_K_WIDTH), F32)
    wa_pad = wa_pad.at[0, :GATE_RANK].set(gla_wa_fwd[0]).at[1, GATE_RANK:2 * GATE_RANK].set(gla_wa_bwd[0])
    p = {
        "norm1_g": norm1_g[0].reshape(1, d),
        "norm2_g": norm2_g[0].reshape(1, d),
        "final_g": final_g.reshape(1, d),
        "w_in": w_in0.astype(BF16),
        "w_gate": jnp.pad(w_in0[:, OFF_GATE:], ((0, 0), (0, GATE_PAD - 2 * GATE_RANK))).astype(BF16),
        "w_out": w_out[0],
        "w_mlp_in": w_mlp_in[0],
        "w_mlp_out": w_mlp_out[0],
        "to_cast": ("w_mlp_in", "w_mlp_out", "w_out"),
        "wa_pad": wa_pad.astype(BF16),
        "ba": jnp.stack([gla_ba_fwd[0], gla_ba_bwd[0]]).reshape(2, 1, GLA_K_WIDTH),
        "gla_norm_g": gla_norm_g[0].reshape(1, GLA_DV),
        "rel_bias": rel_bias,
        "sink": attn_sink[0].reshape(1, N_ATTN_HEADS),
        "bucket": jnp.asarray(_t5_bucket_table()),
        "tri": jnp.asarray(_gla_tri_mats()),
    }
    y_prompt = _trunk_group(x_prompt, mod3, 0, p)
    y_sample = _trunk_group(x_sample, mod3, bp, p)
    return (y_prompt, y_sample)
```

```python
import functools
import math

import numpy as np
import jax
import jax.numpy as jnp
from jax import lax
from jax.experimental import pallas as pl
from jax.experimental.pallas import tpu as pltpu

F32 = jnp.float32
BF16 = jnp.bfloat16

HEAD_DIM = 128
N_ATTN_HEADS = 16
N_KV_HEADS = 4
GQA_GROUP = N_ATTN_HEADS // N_KV_HEADS
ATTN_WIDTH = N_ATTN_HEADS * HEAD_DIM
KV_WIDTH = N_KV_HEADS * HEAD_DIM
WINDOW = 128
BLOCK = 128
GLA_DV = 256
GLA_DK = 128
N_GLA_HEADS = 8
GLA_K_WIDTH = N_GLA_HEADS * GLA_DK
GLA_V_WIDTH = N_GLA_HEADS * GLA_DV
GATE_RANK = 16
GATE_TEMP = 16.0
N_BUCKETS = 32
MAX_DISTANCE = 128
EPS = 1e-6
NEG_INF = -1e30
LOG2E = math.log2(math.e)

OFF_QA = 0
OFF_KA = OFF_QA + ATTN_WIDTH
OFF_VA = OFF_KA + KV_WIDTH
OFF_QG = OFF_VA + KV_WIDTH
OFF_KG = OFF_QG + GLA_K_WIDTH
OFF_VG = OFF_KG + GLA_K_WIDTH
OFF_RG = OFF_VG + GLA_V_WIDTH
OFF_GATE = OFF_RG + GLA_V_WIDTH
PROJ_WIDTH = OFF_GATE
GATE_PAD = 128

GLA_CHUNK = 64
GLA_BLOCK = 256
GLA_SLAB = 64
GLA_UNROLL = 2
INPROJ_SLOTS = 3
ATTN_QBLOCKS = 4
INPROJ_HALVES = 2
MLP_HALVES = 2
CAST_ROWS, CAST_COLS = 128, 4096
SUBLANES = 8
LANES = 128
NORM_ROWS = 16
RSQRT_GROUPS = 8

VMEM_LIMIT = 60 * 1024 * 1024


def _cparams(n_axes, vmem=VMEM_LIMIT):
    return pltpu.CompilerParams(dimension_semantics=("arbitrary",) * n_axes,
                                vmem_limit_bytes=vmem)


def _dot(a, b):
    return jnp.dot(a, b, preferred_element_type=F32)


def _dot_nt(a, b):
    return lax.dot_general(a, b, (((1,), (1,)), ((), ())), preferred_element_type=F32)


def _dot_tn(a, b):
    return lax.dot_general(a, b, (((0,), (0,)), ((), ())), preferred_element_type=F32)


def _silu(x):
    return x * (1.0 / (1.0 + jnp.exp(-x)))


def _row_loop(n_rows, rows_per, body):
    def step(i, carry):
        body(pl.multiple_of(i * rows_per, rows_per))
        return carry
    lax.fori_loop(0, n_rows // rows_per, step, 0)


def _ada_kernel(c_ref, w_ref, b_ref, o_ref):
    a = _silu(c_ref[...]).astype(BF16)
    o_ref[...] = _dot(a, w_ref[...].astype(BF16)) + b_ref[...]


def _ada(c_pad, w_ada, b_ada, tn=512):
    rows, d = c_pad.shape
    n = w_ada.shape[1]
    return pl.pallas_call(
        _ada_kernel,
        grid=(n // tn,),
        in_specs=[pl.BlockSpec((rows, d), lambda j: (0, 0)),
                  pl.BlockSpec((d, tn), lambda j: (0, j)),
                  pl.BlockSpec((1, tn), lambda j: (0, j))],
        out_specs=pl.BlockSpec((rows, tn), lambda j: (0, j)),
        out_shape=jax.ShapeDtypeStruct((rows, n), F32),
        compiler_params=_cparams(1),
        name="ada",
    )(c_pad, w_ada, b_ada.reshape(1, n))


def _row_rsqrt(x_ref, r_ref, n_rows, load=None):
    d = x_ref.shape[-1]

    def body(r0):
        for sub in range(RSQRT_GROUPS):
            rows = pl.ds(r0 + sub * SUBLANES, SUBLANES)
            acc = jnp.zeros((SUBLANES, LANES), F32)
            for j in range(d // LANES):
                cols = slice(j * LANES, (j + 1) * LANES)
                x = x_ref[rows, cols] if load is None else load(rows, cols)
                acc = acc + x * x
            r_ref[rows, :] = acc

    _row_loop(n_rows, RSQRT_GROUPS * SUBLANES, body)
    ms = jnp.sum(r_ref[...], axis=-1, keepdims=True) * (1.0 / d)
    r_ref[...] = jnp.broadcast_to(lax.rsqrt(ms + EPS), r_ref.shape)


def _scale_rows(x_ref, o_ref, r_ref, m_ref, n_rows, shift):
    d = x_ref.shape[-1]
    reps = NORM_ROWS // SUBLANES

    def body(r0):
        rows = pl.ds(r0, NORM_ROWS)
        r = r_ref[rows, :]
        for j in range(d // LANES):
            cols = slice(j * LANES, (j + 1) * LANES)
            y = x_ref[rows, cols] * r * jnp.concatenate([m_ref[0, :, cols]] * reps, axis=0)
            if shift:
                y = y + jnp.concatenate([m_ref[1, :, cols]] * reps, axis=0)
            o_ref[rows, cols] = y.astype(o_ref.dtype)

    _row_loop(n_rows, NORM_ROWS, body)


def _modulated_rms_rows(x_ref, h_ref, r_ref, m_ref, g, sc, sh, n_rows):
    d = x_ref.shape[-1]
    m_ref[0] = jnp.broadcast_to(g * (1.0 + sc), (SUBLANES, d))
    m_ref[1] = jnp.broadcast_to(sh, (SUBLANES, d))
    _row_rsqrt(x_ref, r_ref, n_rows)
    _scale_rows(x_ref, h_ref, r_ref, m_ref, n_rows, shift=True)


def _stream_weight_tiles(n_tiles, n_slots, copies, consume, before=None, after=None):
    assert n_tiles % n_slots == 0, "one loop iteration consumes one tile per slot"
    ahead = n_slots - 1

    @pl.when(pl.program_id(0) == 0)
    def _():
        for t in range(ahead):
            for cp in copies(t, t):
                cp.start()

    def ring(k, carry):
        for slot in range(n_slots):
            t = n_slots * k + slot
            for cp in copies(t, slot):
                cp.wait()
            t_next = jnp.where(t + ahead >= n_tiles, t + ahead - n_tiles, t + ahead)
            for cp in copies(t_next, (slot + ahead) % n_slots):
                cp.start()
            if before is not None:
                before(t, slot)
            consume(t, slot)
            if after is not None:
                after(t, slot)
        return carry

    lax.fori_loop(0, n_tiles // n_slots, ring, 0)

    @pl.when(pl.program_id(0) == pl.num_programs(0) - 1)
    def _():
        for t in range(ahead):
            for cp in copies(t, t):
                cp.wait()


def _cast_chunk_plan(shapes, n_grid, n_tiles):
    plan, lo = [], 0
    for rows, cols in shapes:
        assert rows % (n_grid * CAST_ROWS) == 0 and cols % CAST_COLS == 0
        rows_pg = rows // n_grid
        row_chunks = rows_pg // CAST_ROWS
        hi = lo + row_chunks * (cols // CAST_COLS)
        plan.append((lo, hi, row_chunks, rows_pg))
        lo = hi
    assert lo == n_tiles, "every (grid step, ring tile) pair converts exactly one chunk"
    return plan


def _inproj_kernel(*refs, tn, n_grid, n_cast):
    sh_ref, sc_ref, g_ref, wg_ref, x_hbm, w_hbm = refs[:6]
    srcs = refs[6:6 + n_cast]
    og_ref, o_hbm = refs[6 + n_cast:8 + n_cast]
    dsts = refs[8 + n_cast:8 + 2 * n_cast]
    x_buf, h_ref, r_ref, m_ref, w_buf, o_buf, sem_x, sem_w, sem_o = refs[8 + 2 * n_cast:17 + 2 * n_cast]
    n_half, half, _ = x_buf.shape
    n_tiles = o_hbm.shape[1] // tn
    n_slots = w_buf.shape[0]
    i = pl.program_id(0)

    def x_copy(step, hf):
        rows = pl.ds(pl.multiple_of((step * n_half + hf) * half, half), half)
        return pltpu.make_async_copy(x_hbm.at[rows, :], x_buf.at[hf], sem_x.at[hf])

    def w_copies(t, slot):
        return (pltpu.make_async_copy(w_hbm.at[:, pl.ds(t * tn, tn)], w_buf.at[slot], sem_w.at[slot]),)

    def o_copy(step, t, hf, slot):
        rows = pl.ds(pl.multiple_of((step * n_half + hf) * half, half), half)
        cols = pl.ds(pl.multiple_of(t * tn, tn), tn)
        return pltpu.make_async_copy(o_buf.at[slot, hf], o_hbm.at[rows, cols], sem_o.at[slot, hf])

    if n_cast:
        cin, cout, sem_in, sem_out = refs[17 + 2 * n_cast:]
        plan = _cast_chunk_plan([s.shape for s in srcs], n_grid, n_tiles)
        first_window = (pl.ds(0, CAST_ROWS), pl.ds(0, CAST_COLS))

        def for_owner(step, t, fn, extra=True):
            for a, (lo, hi, row_chunks, rows_pg) in enumerate(plan):
                @pl.when((t >= lo) & (t < hi) & extra)
                def _(a=a, lo=lo, row_chunks=row_chunks, rows_pg=rows_pg):
                    j = t - lo
                    row = step * rows_pg + (j % row_chunks) * CAST_ROWS
                    col = (j // row_chunks) * CAST_COLS
                    fn(a, (pl.ds(pl.multiple_of(row, CAST_ROWS), CAST_ROWS),
                           pl.ds(pl.multiple_of(col, CAST_COLS), CAST_COLS)))

        def copy_in(a, window, slot):
            return pltpu.make_async_copy(srcs[a].at[window], cin.at[slot], sem_in.at[slot])

        def copy_out(a, window, slot):
            return pltpu.make_async_copy(cout.at[slot], dsts[a].at[window], sem_out.at[slot])

    @pl.when(i == 0)
    def _():
        for hf in range(n_half):
            x_copy(0, hf).start()
        if n_cast:
            copy_in(0, first_window, 0).start()

    for hf in range(n_half):
        x_copy(i, hf).wait()
        _modulated_rms_rows(x_buf.at[hf], h_ref.at[pl.ds(hf * half, half)], r_ref, m_ref,
                            g_ref[...], sc_ref[0], sh_ref[0], half)

        @pl.when(i + 1 < n_grid)
        def _(hf=hf):
            x_copy(i + 1, hf).start()

    og_ref[...] = _dot(h_ref[...], wg_ref[...])

    def before(t, slot):
        @pl.when(i * n_tiles + t >= n_slots)
        def _():
            for hf in range(n_half):
                o_copy(0, 0, hf, slot).wait()
            if n_cast:
                copy_out(0, first_window, slot).wait()

        if n_cast:
            copy_in(0, first_window, slot).wait()
            wrap = t + 1 == n_tiles
            for_owner(jnp.where(wrap, i + 1, i), jnp.where(wrap, 0, t + 1),
                      lambda a, window: copy_in(a, window, (slot + 1) % n_slots).start(),
                      extra=jnp.logical_not(wrap & (i == n_grid - 1)))

    def consume(t, slot):
        for hf in range(n_half):
            o_buf[slot, hf] = _dot(h_ref[hf * half:(hf + 1) * half, :], w_buf[slot]).astype(o_buf.dtype)
        if n_cast:
            cout[slot] = cin[slot].astype(BF16)

    def after(t, slot):
        for hf in range(n_half):
            o_copy(i, t, hf, slot).start()
        if n_cast:
            for_owner(i, t, lambda a, window: copy_out(a, window, slot).start())

    _stream_weight_tiles(n_tiles, n_slots, w_copies, consume, before, after)

    @pl.when(i == n_grid - 1)
    def _():
        for slot in range(n_slots):
            for hf in range(n_half):
                o_copy(0, 0, hf, slot).wait()
            if n_cast:
                copy_out(0, first_window, slot).wait()


def _in_proj(x2, mod3, b0, seq, norm_g, w_bf, wg_bf, cast=(), half=512, tn=512):
    m, d = x2.shape
    n = min(w_bf.shape[1], PROJ_WIDTH)
    tm = INPROJ_HALVES * half
    assert n % tn == 0 and m % tm == 0 and seq % tm == 0
    n_grid = m // tm
    mod_spec = lambda k: pl.BlockSpec((1, 1, d), lambda i: ((b0 + (i * tm) // seq) * 6 + k, 0, 0))
    hbm = pl.BlockSpec(memory_space=pl.ANY)
    scratch = [pltpu.VMEM((INPROJ_HALVES, half, d), F32), pltpu.VMEM((tm, d), BF16),
               pltpu.VMEM((half, LANES), F32), pltpu.VMEM((2, SUBLANES, d), F32),
               pltpu.VMEM((INPROJ_SLOTS, d, tn), BF16), pltpu.VMEM((INPROJ_SLOTS, INPROJ_HALVES, half, tn), BF16),
               pltpu.SemaphoreType.DMA((INPROJ_HALVES,)), pltpu.SemaphoreType.DMA((INPROJ_SLOTS,)),
               pltpu.SemaphoreType.DMA((INPROJ_SLOTS, INPROJ_HALVES))]
    if cast:
        scratch += [pltpu.VMEM((INPROJ_SLOTS, CAST_ROWS, CAST_COLS), F32),
                    pltpu.VMEM((INPROJ_SLOTS, CAST_ROWS, CAST_COLS), BF16),
                    pltpu.SemaphoreType.DMA((INPROJ_SLOTS,)), pltpu.SemaphoreType.DMA((INPROJ_SLOTS,))]
    gates, proj, *converted = pl.pallas_call(
        functools.partial(_inproj_kernel, tn=tn, n_grid=n_grid, n_cast=len(cast)),
        grid=(n_grid,),
        in_specs=[mod_spec(0), mod_spec(1),
                  pl.BlockSpec((1, d), lambda i: (0, 0)),
                  pl.BlockSpec((d, GATE_PAD), lambda i: (0, 0)),
                  hbm, hbm] + [hbm] * len(cast),
        out_specs=[pl.BlockSpec((tm, GATE_PAD), lambda i: (i, 0)), hbm] + [hbm] * len(cast),
        out_shape=[jax.ShapeDtypeStruct((m, GATE_PAD), F32),
                   jax.ShapeDtypeStruct((m, n), BF16)]
                  + [jax.ShapeDtypeStruct(a.shape, BF16) for a in cast],
        scratch_shapes=scratch,
        compiler_params=_cparams(1),
        name="in_proj_cast" if cast else "in_proj",
    )(mod3, mod3, norm_g, wg_bf, x2, w_bf, *cast)
    return (proj, gates, *converted)


def _t5_bucket_table():
    half = N_BUCKETS // 2
    max_exact = half // 2
    qi = np.arange(BLOCK)[:, None]
    kj = np.arange(3 * BLOCK)[None, :]
    rel = kj - BLOCK - qi
    n = np.abs(rel)
    nf = np.maximum(n, 1).astype(np.float32)
    large = max_exact + (np.log(nf / max_exact) / math.log(MAX_DISTANCE / max_exact)
                         * (half - max_exact)).astype(np.int32)
    large = np.minimum(large, half - 1)
    return (np.where(rel > 0, half, 0) + np.where(n < max_exact, n, large)).astype(np.int32)


def _attn_kernel(rb_ref, sink_ref, bucket_ref, q_ref, k_ref, v_ref, o_ref, tbl_ref, *, nb):
    @pl.when((pl.program_id(0) == 0) & (pl.program_id(1) == 0))
    def _():
        qi = lax.broadcasted_iota(jnp.int32, (BLOCK, 3 * BLOCK), 0)
        kj = lax.broadcasted_iota(jnp.int32, (BLOCK, 3 * BLOCK), 1)
        in_band = jnp.abs(kj - BLOCK - qi) <= WINDOW
        bucket = bucket_ref[...]

        def per_head(h, carry):
            def per_bucket(bk, acc):
                return jnp.where(bucket == bk, rb_ref[bk, h], acc)
            acc = lax.fori_loop(0, N_BUCKETS, per_bucket, jnp.zeros((BLOCK, 3 * BLOCK), F32)) * LOG2E
            for v in range(4):
                keep = in_band
                if v & 1:
                    keep = keep & (kj >= BLOCK)
                if v & 2:
                    keep = keep & (kj < 2 * BLOCK)
                tbl_ref[v, h] = jnp.where(keep, acc, NEG_INF)
            return carry

        lax.fori_loop(0, N_ATTN_HEADS, per_head, 0)

    scale2 = HEAD_DIM ** -0.5 * LOG2E

    for sub in range(ATTN_QBLOCKS):
        n = ATTN_QBLOCKS * pl.program_id(1) + sub
        qrows = slice(sub * BLOCK, (sub + 1) * BLOCK)
        r_prev = pl.multiple_of(jnp.maximum(n - 1, 0) * BLOCK, BLOCK)
        r_cur = pl.multiple_of(n * BLOCK, BLOCK)
        r_next = pl.multiple_of(jnp.minimum(n + 1, nb - 1) * BLOCK, BLOCK)
        variant = (n == 0).astype(jnp.int32) + 2 * (n == nb - 1).astype(jnp.int32)
        for g in range(N_KV_HEADS):
            kc = slice(g * HEAD_DIM, (g + 1) * HEAD_DIM)
            k3 = jnp.concatenate([k_ref[0, pl.ds(r_prev, BLOCK), kc],
                                  k_ref[0, pl.ds(r_cur, BLOCK), kc],
                                  k_ref[0, pl.ds(r_next, BLOCK), kc]], axis=0)
            v3 = jnp.concatenate([v_ref[0, pl.ds(r_prev, BLOCK), kc],
                                  v_ref[0, pl.ds(r_cur, BLOCK), kc],
                                  v_ref[0, pl.ds(r_next, BLOCK), kc]], axis=0)
            heads = [g * GQA_GROUP + j for j in range(GQA_GROUP)]
            qs = jnp.concatenate([q_ref[0, qrows, h * HEAD_DIM:(h + 1) * HEAD_DIM] for h in heads], axis=0)
            s = _dot_nt(qs, k3)
            ps, inv_dens = [], []
            for j, h in enumerate(heads):
                sj = s[j * BLOCK:(j + 1) * BLOCK] * scale2 + tbl_ref[variant, h]
                sink = sink_ref[0, h] * LOG2E
                m = jnp.maximum(jnp.max(sj, axis=-1, keepdims=True), sink)
                p = jnp.exp2(sj - m)
                den = jnp.sum(p, axis=-1, keepdims=True) + jnp.exp2(sink - m)
                ps.append(p.astype(BF16))
                inv_dens.append(1.0 / den)
            o = _dot(jnp.concatenate(ps, axis=0), v3)
            for j, h in enumerate(heads):
                o_ref[0, qrows, h * HEAD_DIM:(h + 1) * HEAD_DIM] = (
                    o[j * BLOCK:(j + 1) * BLOCK] * inv_dens[j]).astype(o_ref.dtype)


def _attention(proj3, rel_bias, sink, bucket):
    bsz, seq, _ = proj3.shape
    nb = seq // BLOCK
    smem = pl.BlockSpec(memory_space=pltpu.SMEM)
    return pl.pallas_call(
        functools.partial(_attn_kernel, nb=nb),
        grid=(bsz, nb // ATTN_QBLOCKS),
        in_specs=[smem, smem,
                  pl.BlockSpec((BLOCK, 3 * BLOCK), lambda b, n: (0, 0)),
                  pl.BlockSpec((1, ATTN_QBLOCKS * BLOCK, ATTN_WIDTH), lambda b, n: (b, n, OFF_QA // ATTN_WIDTH)),
                  pl.BlockSpec((1, seq, KV_WIDTH), lambda b, n: (b, 0, OFF_KA // KV_WIDTH)),
                  pl.BlockSpec((1, seq, KV_WIDTH), lambda b, n: (b, 0, OFF_VA // KV_WIDTH))],
        out_specs=pl.BlockSpec((1, ATTN_QBLOCKS * BLOCK, ATTN_WIDTH), lambda b, n: (b, n, 0)),
        out_shape=jax.ShapeDtypeStruct((bsz, seq, ATTN_WIDTH), BF16),
        scratch_shapes=[pltpu.VMEM((4, N_ATTN_HEADS, BLOCK, 3 * BLOCK), F32)],
        compiler_params=_cparams(2),
        name="window_attn",
    )(rel_bias, sink, bucket, proj3, proj3, proj3)


def _gla_tri_mats():
    t = np.arange(GLA_BLOCK)
    same = (t[:, None] // GLA_CHUNK) == (t[None, :] // GLA_CHUNK)
    fwd = same & (t[None, :] <= t[:, None])
    bwd = same & (t[None, :] >= t[:, None])
    return np.stack([fwd, bwd]).astype(np.float32)


def _gla_kernel(tri_ref, msk_ref, q_ref, k_ref, v_ref, r_ref, gt_ref, wa_ref, ba_ref, ng_ref, o_ref,
                bcum_ref, qt_ref, stb_ref, st_ref, u_ref, acc_ref, *, seq):
    c = GLA_CHUNK
    cpb = GLA_BLOCK // c
    nblk = seq // GLA_BLOCK
    scale = GLA_DK ** -0.5
    anchor = (c // 2 - 1, c // 2)
    last = (c - 1, 0)

    def gates(d, bi):
        rows = pl.ds(pl.multiple_of(bi * GLA_BLOCK, GLA_BLOCK), GLA_BLOCK)
        z = _dot(gt_ref[0, rows, :].astype(BF16), wa_ref[d]) + ba_ref[d]
        lg = (jnp.minimum(z, 0.0) - jnp.log(1.0 + jnp.exp(-jnp.abs(z)))) * (1.0 / GATE_TEMP)
        hi = lg.astype(BF16)
        lo = (lg - hi.astype(F32)).astype(BF16)
        cum = _dot(tri_ref[d], jnp.concatenate([hi, lo], axis=1))
        bcum_ref[d, rows, :] = cum[:, :GLA_DK] + cum[:, GLA_DK:]

    st_ref[...] = jnp.zeros_like(st_ref)
    for sub in range(GLA_UNROLL):
        gates(0, sub)
        gates(1, nblk - 1 - sub)

    def scan_blocks(i, carry):
        for sub in range(GLA_UNROLL):
            ib = i * GLA_UNROLL + sub
            for d, bi in ((0, ib), (1, nblk - 1 - ib)):
                r0 = pl.multiple_of(bi * GLA_BLOCK, GLA_BLOCK)
                lanes = slice(d * GLA_DK, (d + 1) * GLA_DK)
                decays, qps, kps = [], [], []
                for ci in range(cpb):
                    rows = pl.ds(r0 + ci * c, c)
                    bc = bcum_ref[d, rows, :]
                    b_a = bc[anchor[d]:anchor[d] + 1]
                    b_l = bc[last[d]:last[d] + 1]
                    qpf = q_ref[0, rows, :].astype(F32) * scale * jnp.exp(bc - b_a)
                    kpf = k_ref[0, rows, :].astype(F32) * jnp.exp(b_a - bc)
                    qps.append(qpf.astype(BF16))
                    kps.append(kpf.astype(BF16))
                    qt_ref[rows, lanes] = (qpf * jnp.exp(b_a)).astype(BF16)
                    kd = (kpf * jnp.exp(b_l - b_a)).astype(BF16)
                    u_ref[d, sub * cpb + ci] = _dot_tn(v_ref[0, rows, :], kd)
                    decays.append(jnp.exp(b_l))
                rows = pl.ds(r0, GLA_BLOCK)
                s = _dot_nt(jnp.concatenate(qps, axis=0), jnp.concatenate(kps, axis=0))
                a = jnp.where(msk_ref[d] > 0.5, s, 0.0).astype(BF16)
                acc_ref[d, rows, :] = _dot(a, v_ref[0, rows, :])
                for s in range(GLA_DV // GLA_SLAB):
                    sl = slice(s * GLA_SLAB, (s + 1) * GLA_SLAB)
                    st = st_ref[d, sl, :]
                    for ci in (range(cpb) if d == 0 else reversed(range(cpb))):
                        stb_ref[bi * cpb + ci, sl, lanes] = st.astype(BF16)
                        st = st * decays[ci] + u_ref[d, sub * cpb + ci, sl, :]
                    st_ref[d, sl, :] = st
        for sub in range(GLA_UNROLL):
            ib = (i + 1) * GLA_UNROLL + sub
            gates(0, jnp.minimum(ib, nblk - 1))
            gates(1, jnp.maximum(nblk - 1 - ib, 0))
        return carry

    lax.fori_loop(0, nblk // GLA_UNROLL, scan_blocks, 0)

    ng = ng_ref[...]

    def post(r0):
        for ci in range(2 * cpb):
            rows = pl.ds(r0 + ci * c, c)
            o = (acc_ref[0, rows, :] + acc_ref[1, rows, :]
                 + _dot_nt(qt_ref[rows, :], stb_ref[lax.div(r0, c) + ci]))
            ms = jnp.sum(o * o, axis=-1, keepdims=True) * (1.0 / GLA_DV)
            on = o * lax.rsqrt(ms + EPS) * ng
            o_ref[0, rows, :] = (on * _silu(r_ref[0, rows, :].astype(F32))).astype(o_ref.dtype)

    _row_loop(seq, 2 * GLA_BLOCK, post)


def _gla(proj3, gates3, tri, wa_pad, ba, norm_g):
    bsz, seq, _ = proj3.shape
    nc = seq // GLA_CHUNK
    whole = lambda shape: pl.BlockSpec(shape, lambda b, h: (0,) * len(shape))
    return pl.pallas_call(
        functools.partial(_gla_kernel, seq=seq),
        grid=(bsz, N_GLA_HEADS),
        in_specs=[whole((2, GLA_BLOCK, GLA_BLOCK)),
                  whole((2, GLA_BLOCK, GLA_BLOCK)),
                  pl.BlockSpec((1, seq, GLA_DK), lambda b, h: (b, 0, OFF_QG // GLA_DK + h)),
                  pl.BlockSpec((1, seq, GLA_DK), lambda b, h: (b, 0, OFF_KG // GLA_DK + h)),
                  pl.BlockSpec((1, seq, GLA_DV), lambda b, h: (b, 0, OFF_VG // GLA_DV + h)),
                  pl.BlockSpec((1, seq, GLA_DV), lambda b, h: (b, 0, OFF_RG // GLA_DV + h)),
                  pl.BlockSpec((1, seq, GATE_PAD), lambda b, h: (b, 0, 0)),
                  pl.BlockSpec((2, GATE_PAD, GLA_DK), lambda b, h: (0, 0, h)),
                  pl.BlockSpec((2, 1, GLA_DK), lambda b, h: (0, 0, h)),
                  whole((1, GLA_DV))],
        out_specs=pl.BlockSpec((1, seq, GLA_DV), lambda b, h: (b, 0, h)),
        out_shape=jax.ShapeDtypeStruct((bsz, seq, GLA_V_WIDTH), BF16),
        scratch_shapes=[pltpu.VMEM((2, seq, GLA_DK), F32),
                        pltpu.VMEM((seq, 2 * GLA_DK), BF16),
                        pltpu.VMEM((nc, GLA_DV, 2 * GLA_DK), BF16),
                        pltpu.VMEM((2, GLA_DV, GLA_DK), F32),
                        pltpu.VMEM((2, GLA_UNROLL * GLA_BLOCK // GLA_CHUNK, GLA_DV, GLA_DK), F32),
                        pltpu.VMEM((2, seq, GLA_DV), F32)],
        compiler_params=_cparams(2),
        name="gla",
    )(tri.astype(BF16), tri, proj3, proj3, proj3, proj3, gates3, wa_pad, ba, norm_g)


def _outproj_kernel(a_ref, g_ref, wa_ref, wg_ref, x_ref, g1_ref, o_ref):
    acc = _dot(a_ref[...], wa_ref[...]) + _dot(g_ref[...], wg_ref[...])
    o_ref[...] = x_ref[...] + g1_ref[0] * acc


def _out_proj(attn2, gla2, w_bf, x2, mod3, b0, seq, tm=1024, tn=1024):
    m, d = x2.shape
    ka = attn2.shape[1]
    kg = gla2.shape[1]
    return pl.pallas_call(
        _outproj_kernel,
        grid=(m // tm, d // tn),
        in_specs=[pl.BlockSpec((tm, ka), lambda i, j: (i, 0)),
                  pl.BlockSpec((tm, kg), lambda i, j: (i, 0)),
                  pl.BlockSpec((ka, tn), lambda i, j: (0, j)),
                  pl.BlockSpec((kg, tn), lambda i, j: (ka // kg, j)),
                  pl.BlockSpec((tm, tn), lambda i, j: (i, j)),
                  pl.BlockSpec((1, 1, tn), lambda i, j: ((b0 + (i * tm) // seq) * 6 + 2, 0, j))],
        out_specs=pl.BlockSpec((tm, tn), lambda i, j: (i, j)),
        out_shape=jax.ShapeDtypeStruct((m, d), F32),
        compiler_params=_cparams(2),
        name="out_proj",
    )(attn2, gla2, w_bf, w_bf, x2, mod3)


def _mlp_kernel(sh_ref, sc_ref, g2_ref, ng_ref, fg_ref, x_hbm, w1_hbm, w2_hbm, y_hbm,
                x_buf, acc, h_ref, r_ref, m_ref, w1_buf, w2_buf, sem_x, sem_w, sem_y, *, n_grid, tf, tn):
    n_half, half, d = x_buf.shape
    i = pl.program_id(0)

    def x_copy(step, hf):
        rows = pl.ds(pl.multiple_of((step * n_half + hf) * half, half), half)
        return pltpu.make_async_copy(x_hbm.at[rows, :], x_buf.at[hf], sem_x.at[hf])

    def y_copy(step, hf):
        rows = pl.ds(pl.multiple_of((step * n_half + hf) * half, half), half)
        return pltpu.make_async_copy(acc.at[hf], y_hbm.at[rows, :], sem_y.at[hf])

    def w_copies(f, slot):
        return (pltpu.make_async_copy(w1_hbm.at[:, pl.ds(f * tf, tf)], w1_buf.at[slot], sem_w.at[0, slot]),
                pltpu.make_async_copy(w2_hbm.at[pl.ds(f * tf, tf), :], w2_buf.at[slot], sem_w.at[1, slot]))

    @pl.when(i == 0)
    def _():
        for hf in range(n_half):
            x_copy(0, hf).start()

    for hf in range(n_half):
        x_copy(i, hf).wait()
        _modulated_rms_rows(x_buf.at[hf], h_ref.at[pl.ds(hf * half, half)], r_ref, m_ref,
                            ng_ref[...], sc_ref[0], sh_ref[0], half)

        @pl.when(i > 0)
        def _(hf=hf):
            y_copy(i - 1, hf).wait()

        acc[hf] = jnp.zeros((half, d), F32)

    def consume(f, slot):
        for hf in range(n_half):
            u = _dot(h_ref[hf * half:(hf + 1) * half, :], w1_buf[slot])
            u = jnp.square(jnp.maximum(u, 0.0)).astype(BF16)
            for c in range(d // tn):
                cols = slice(c * tn, (c + 1) * tn)
                acc[hf, :, cols] += _dot(u, w2_buf[slot, :, cols])

    _stream_weight_tiles(w1_hbm.shape[1] // tf, w1_buf.shape[0], w_copies, consume)

    m_ref[0] = jnp.broadcast_to(fg_ref[...], (SUBLANES, d))
    m_ref[1] = jnp.broadcast_to(g2_ref[0], (SUBLANES, d))
    for hf in range(n_half):
        x_half, o_half = x_buf.at[hf], acc.at[hf]

        def residual(rows, cols, x_half=x_half, o_half=o_half):
            x2 = x_half[rows, cols] + m_ref[1, :, cols] * o_half[rows, cols]
            o_half[rows, cols] = x2
            return x2

        _row_rsqrt(o_half, r_ref, half, load=residual)
        _scale_rows(o_half, o_half, r_ref, m_ref, half, shift=False)
        y_copy(i, hf).start()

        @pl.when(i + 1 < n_grid)
        def _(hf=hf):
            x_copy(i + 1, hf).start()

    @pl.when(i == n_grid - 1)
    def _():
        for hf in range(n_half):
            y_copy(i, hf).wait()


def _mlp(x1, mod3, b0, seq, norm_g, final_g, w1_bf, w2_bf, half=512, tf=512, tn=512):
    m, d = x1.shape
    tm = MLP_HALVES * half
    assert m % tm == 0 and seq % tm == 0
    n_grid = m // tm
    mod_spec = lambda k: pl.BlockSpec((1, 1, d), lambda i: ((b0 + (i * tm) // seq) * 6 + k, 0, 0))
    hbm = pl.BlockSpec(memory_space=pl.ANY)
    return pl.pallas_call(
        functools.partial(_mlp_kernel, n_grid=n_grid, tf=tf, tn=tn),
        grid=(n_grid,),
        in_specs=[mod_spec(3), mod_spec(4), mod_spec(5),
                  pl.BlockSpec((1, d), lambda i: (0, 0)),
                  pl.BlockSpec((1, d), lambda i: (0, 0)),
                  hbm, hbm, hbm],
        out_specs=hbm,
        out_shape=jax.ShapeDtypeStruct((m, d), F32),
        scratch_shapes=[pltpu.VMEM((MLP_HALVES, half, d), F32), pltpu.VMEM((MLP_HALVES, half, d), F32),
                        pltpu.VMEM((tm, d), BF16), pltpu.VMEM((half, LANES), F32),
                        pltpu.VMEM((2, SUBLANES, d), F32),
                        pltpu.VMEM((2, d, tf), BF16), pltpu.VMEM((2, tf, d), BF16),
                        pltpu.SemaphoreType.DMA((MLP_HALVES,)), pltpu.SemaphoreType.DMA((2, 2)),
                        pltpu.SemaphoreType.DMA((MLP_HALVES,))],
        compiler_params=_cparams(1),
        name="mlp",
    )(mod3, mod3, mod3, norm_g, final_g, x1, w1_bf, w2_bf)


def _trunk_group(x, mod3, b0, p):
    bsz, seq, d = x.shape
    x2 = x.reshape(bsz * seq, d)
    names = p.pop("to_cast", ())
    proj, gates, *converted = _in_proj(x2, mod3, b0, seq, p["norm1_g"], p["w_in"], p["w_gate"],
                                       cast=tuple(p[k] for k in names))
    p.update(zip(names, converted))
    proj3 = proj.reshape(bsz, seq, PROJ_WIDTH)
    gates3 = gates.reshape(bsz, seq, GATE_PAD)
    attn = _attention(proj3, p["rel_bias"], p["sink"], p["bucket"])
    gla = _gla(proj3, gates3, p["tri"], p["wa_pad"], p["ba"], p["gla_norm_g"])
    x1 = _out_proj(attn.reshape(bsz * seq, ATTN_WIDTH), gla.reshape(bsz * seq, GLA_V_WIDTH),
                   p["w_out"], x2, mod3, b0, seq)
    y = _mlp(x1, mod3, b0, seq, p["norm2_g"], p["final_g"], p["w_mlp_in"], p["w_mlp_out"])
    return y.reshape(bsz, seq, d)


def kernel(x_prompt, x_sample, c_prompt, c_sample, w_ada, b_ada, norm1_g, w_in, gla_wa_fwd, gla_ba_fwd, gla_wa_bwd, gla_ba_bwd, gla_norm_g, attn_sink, rel_bias, w_out, norm2_g, w_mlp_in, w_mlp_out, final_g):
    assert w_ada.shape[0] == 1, "single-layer trunk"
    d = x_prompt.shape[-1]
    bp, bs = c_prompt.shape[0], c_sample.shape[0]

    rows = -(-(bp + bs) // 16) * 16
    c_pad = jnp.zeros((rows, d), F32).at[:bp].set(c_prompt).at[bp:bp + bs].set(c_sample)
    mod = _ada(c_pad, w_ada[0], b_ada[0])
    mod3 = mod.reshape(rows * 6, 1, d)

    w_in0 = w_in[0]
    wa_pad = jnp.zeros((2, GATE_PAD, GLA_K_WIDTH), F32)
    wa_pad = wa_pad.at[0, :GATE_RANK].set(gla_wa_fwd[0]).at[1, GATE_RANK:2 * GATE_RANK].set(gla_wa_bwd[0])
    p = {
        "norm1_g": norm1_g[0].reshape(1, d),
        "norm2_g": norm2_g[0].reshape(1, d),
        "final_g": final_g.reshape(1, d),
        "w_in": w_in0.astype(BF16),
        "w_gate": jnp.pad(w_in0[:, OFF_GATE:], ((0, 0), (0, GATE_PAD - 2 * GATE_RANK))).astype(BF16),
        "w_out": w_out[0],
        "w_mlp_in": w_mlp_in[0],
        "w_mlp_out": w_mlp_out[0],
        "to_cast": ("w_mlp_in", "w_mlp_out", "w_out"),
        "wa_pad": wa_pad.astype(BF16),
        "ba": jnp.stack([gla_ba_fwd[0], gla_ba_bwd[0]]).reshape(2, 1, GLA_K_WIDTH),
        "gla_norm_g": gla_norm_g[0].reshape(1, GLA_DV),
        "rel_bias": rel_bias,
        "sink": attn_sink[0].reshape(1, N_ATTN_HEADS),
        "bucket": jnp.asarray(_t5_bucket_table()),
        "tri": jnp.asarray(_gla_tri_mats()),
    }
    y_prompt = _trunk_group(x_prompt, mod3, 0, p)
    y_sample = _trunk_group(x_sample, mod3, bp, p)
    return (y_prompt, y_sample)
```

```python
import functools
import math

import numpy as np
import jax
import jax.numpy as jnp
from jax import lax
from jax.experimental import pallas as pl
from jax.experimental.pallas import tpu as pltpu

F32 = jnp.float32
BF16 = jnp.bfloat16

HEAD_DIM = 128
N_ATTN_HEADS = 16
N_KV_HEADS = 4
GQA_GROUP = N_ATTN_HEADS // N_KV_HEADS
ATTN_WIDTH = N_ATTN_HEADS * HEAD_DIM
KV_WIDTH = N_KV_HEADS * HEAD_DIM
WINDOW = 128
BLOCK = 128
GLA_DV = 256
GLA_DK = 128
N_GLA_HEADS = 8
GLA_K_WIDTH = N_GLA_HEADS * GLA_DK
GLA_V_WIDTH = N_GLA_HEADS * GLA_DV
GATE_RANK = 16
GATE_TEMP = 16.0
N_BUCKETS = 32
MAX_DISTANCE = 128
EPS = 1e-6
NEG_INF = -1e30
LOG2E = math.log2(math.e)

OFF_QA = 0
OFF_KA = OFF_QA + ATTN_WIDTH
OFF_VA = OFF_KA + KV_WIDTH
OFF_QG = OFF_VA + KV_WIDTH
OFF_KG = OFF_QG + GLA_K_WIDTH
OFF_VG = OFF_KG + GLA_K_WIDTH
OFF_RG = OFF_VG + GLA_V_WIDTH
OFF_GATE = OFF_RG + GLA_V_WIDTH
PROJ_WIDTH = OFF_GATE
GATE_PAD = 128

GLA_CHUNK = 64
GLA_BLOCK = 256
GLA_SLAB = 64
GLA_UNROLL = 2
GLA_POST_BLOCKS = 4
INPROJ_SLOTS = 3
ATTN_QBLOCKS = 4
INPROJ_HALVES = 2
MLP_HALVES = 2
CAST_ROWS, CAST_COLS = 128, 4096
SUBLANES = 8
LANES = 128
NORM_ROWS = 16
RSQRT_GROUPS = 8

V7X_VMEM_BYTES = 64 * 1024 * 1024
VMEM_LIMIT = V7X_VMEM_BYTES * 15 // 16


def _cparams(n_axes, vmem=VMEM_LIMIT):
    return pltpu.CompilerParams(dimension_semantics=("arbitrary",) * n_axes,
                                vmem_limit_bytes=vmem)


def _dot(a, b):
    return jnp.dot(a, b, preferred_element_type=F32)


def _dot_nt(a, b):
    return lax.dot_general(a, b, (((1,), (1,)), ((), ())), preferred_element_type=F32)


def _dot_tn(a, b):
    return lax.dot_general(a, b, (((0,), (0,)), ((), ())), preferred_element_type=F32)


def _silu(x):
    return x * (1.0 / (1.0 + jnp.exp(-x)))


def _row_loop(n_rows, rows_per, body):
    def step(i, carry):
        body(pl.multiple_of(i * rows_per, rows_per))
        return carry
    lax.fori_loop(0, n_rows // rows_per, step, 0)


def _ada_kernel(c_ref, w_ref, b_ref, o_ref):
    a = _silu(c_ref[...]).astype(BF16)
    o_ref[...] = _dot(a, w_ref[...].astype(BF16)) + b_ref[...]


def _ada(c_pad, w_ada, b_ada, tn=512):
    rows, d = c_pad.shape
    n = w_ada.shape[1]
    return pl.pallas_call(
        _ada_kernel,
        grid=(n // tn,),
        in_specs=[pl.BlockSpec((rows, d), lambda j: (0, 0)),
                  pl.BlockSpec((d, tn), lambda j: (0, j)),
                  pl.BlockSpec((1, tn), lambda j: (0, j))],
        out_specs=pl.BlockSpec((rows, tn), lambda j: (0, j)),
        out_shape=jax.ShapeDtypeStruct((rows, n), F32),
        compiler_params=_cparams(1),
        name="ada",
    )(c_pad, w_ada, b_ada.reshape(1, n))


def _row_rsqrt(x_ref, r_ref, n_rows, load=None):
    d = x_ref.shape[-1]

    def body(r0):
        for sub in range(RSQRT_GROUPS):
            rows = pl.ds(r0 + sub * SUBLANES, SUBLANES)
            acc = jnp.zeros((SUBLANES, LANES), F32)
            for j in range(d // LANES):
                cols = slice(j * LANES, (j + 1) * LANES)
                x = x_ref[rows, cols] if load is None else load(rows, cols)
                acc = acc + x * x
            r_ref[rows, :] = acc

    _row_loop(n_rows, RSQRT_GROUPS * SUBLANES, body)
    ms = jnp.sum(r_ref[...], axis=-1, keepdims=True) * (1.0 / d)
    r_ref[...] = jnp.broadcast_to(lax.rsqrt(ms + EPS), r_ref.shape)


def _scale_rows(x_ref, o_ref, r_ref, m_ref, n_rows, shift):
    d = x_ref.shape[-1]
    reps = NORM_ROWS // SUBLANES

    def body(r0):
        rows = pl.ds(r0, NORM_ROWS)
        r = r_ref[rows, :]
        for j in range(d // LANES):
            cols = slice(j * LANES, (j + 1) * LANES)
            y = x_ref[rows, cols] * r * jnp.concatenate([m_ref[0, :, cols]] * reps, axis=0)
            if shift:
                y = y + jnp.concatenate([m_ref[1, :, cols]] * reps, axis=0)
            o_ref[rows, cols] = y.astype(o_ref.dtype)

    _row_loop(n_rows, NORM_ROWS, body)


def _modulated_rms_rows(x_ref, h_ref, r_ref, m_ref, g, sc, sh, n_rows):
    d = x_ref.shape[-1]
    m_ref[0] = jnp.broadcast_to(g * (1.0 + sc), (SUBLANES, d))
    m_ref[1] = jnp.broadcast_to(sh, (SUBLANES, d))
    _row_rsqrt(x_ref, r_ref, n_rows)
    _scale_rows(x_ref, h_ref, r_ref, m_ref, n_rows, shift=True)


def _stream_weight_tiles(n_tiles, n_slots, copies, consume, before=None, after=None):
    assert n_tiles % n_slots == 0, "one loop iteration consumes one tile per slot"
    ahead = n_slots - 1

    @pl.when(pl.program_id(0) == 0)
    def _():
        for t in range(ahead):
            for cp in copies(t, t):
                cp.start()

    def ring(k, carry):
        for slot in range(n_slots):
            t = n_slots * k + slot
            for cp in copies(t, slot):
                cp.wait()
            t_next = jnp.where(t + ahead >= n_tiles, t + ahead - n_tiles, t + ahead)
            for cp in copies(t_next, (slot + ahead) % n_slots):
                cp.start()
            if before is not None:
                before(t, slot)
            consume(t, slot)
            if after is not None:
                after(t, slot)
        return carry

    lax.fori_loop(0, n_tiles // n_slots, ring, 0)

    @pl.when(pl.program_id(0) == pl.num_programs(0) - 1)
    def _():
        for t in range(ahead):
            for cp in copies(t, t):
                cp.wait()


def _cast_chunk_plan(shapes, n_grid, n_tiles):
    plan, lo = [], 0
    for rows, cols in shapes:
        assert rows % (n_grid * CAST_ROWS) == 0 and cols % CAST_COLS == 0
        rows_pg = rows // n_grid
        row_chunks = rows_pg // CAST_ROWS
        hi = lo + row_chunks * (cols // CAST_COLS)
        plan.append((lo, hi, row_chunks, rows_pg))
        lo = hi
    assert lo == n_tiles, "every (grid step, ring tile) pair converts exactly one chunk"
    return plan


def _inproj_kernel(*refs, tn, n_grid, n_cast):
    sh_ref, sc_ref, g_ref, wg_ref, x_hbm, w_hbm = refs[:6]
    srcs = refs[6:6 + n_cast]
    og_ref, o_hbm = refs[6 + n_cast:8 + n_cast]
    dsts = refs[8 + n_cast:8 + 2 * n_cast]
    x_buf, h_ref, r_ref, m_ref, w_buf, o_buf, sem_x, sem_w, sem_o = refs[8 + 2 * n_cast:17 + 2 * n_cast]
    n_half, half, _ = x_buf.shape
    n_tiles = o_hbm.shape[1] // tn
    n_slots = w_buf.shape[0]
    i = pl.program_id(0)

    def x_copy(step, hf):
        rows = pl.ds(pl.multiple_of((step * n_half + hf) * half, half), half)
        return pltpu.make_async_copy(x_hbm.at[rows, :], x_buf.at[hf], sem_x.at[hf])

    def w_copies(t, slot):
        return (pltpu.make_async_copy(w_hbm.at[:, pl.ds(t * tn, tn)], w_buf.at[slot], sem_w.at[slot]),)

    def o_copy(step, t, hf, slot):
        rows = pl.ds(pl.multiple_of((step * n_half + hf) * half, half), half)
        cols = pl.ds(pl.multiple_of(t * tn, tn), tn)
        return pltpu.make_async_copy(o_buf.at[slot, hf], o_hbm.at[rows, cols], sem_o.at[slot, hf])

    if n_cast:
        cin, cout, sem_in, sem_out = refs[17 + 2 * n_cast:]
        plan = _cast_chunk_plan([s.shape for s in srcs], n_grid, n_tiles)
        first_window = (pl.ds(0, CAST_ROWS), pl.ds(0, CAST_COLS))

        def for_owner(step, t, fn, extra=True):
            for a, (lo, hi, row_chunks, rows_pg) in enumerate(plan):
                @pl.when((t >= lo) & (t < hi) & extra)
                def _(a=a, lo=lo, row_chunks=row_chunks, rows_pg=rows_pg):
                    j = t - lo
                    row = step * rows_pg + (j % row_chunks) * CAST_ROWS
                    col = (j // row_chunks) * CAST_COLS
                    fn(a, (pl.ds(pl.multiple_of(row, CAST_ROWS), CAST_ROWS),
                           pl.ds(pl.multiple_of(col, CAST_COLS), CAST_COLS)))

        def copy_in(a, window, slot):
            return pltpu.make_async_copy(srcs[a].at[window], cin.at[slot], sem_in.at[slot])

        def copy_out(a, window, slot):
            return pltpu.make_async_copy(cout.at[slot], dsts[a].at[window], sem_out.at[slot])

    @pl.when(i == 0)
    def _():
        for hf in range(n_half):
            x_copy(0, hf).start()
        if n_cast:
            copy_in(0, first_window, 0).start()

    for hf in range(n_half):
        x_copy(i, hf).wait()
        _modulated_rms_rows(x_buf.at[hf], h_ref.at[pl.ds(hf * half, half)], r_ref, m_ref,
                            g_ref[...], sc_ref[0], sh_ref[0], half)

        @pl.when(i + 1 < n_grid)
        def _(hf=hf):
            x_copy(i + 1, hf).start()

    og_ref[...] = _dot(h_ref[...], wg_ref[...])

    def before(t, slot):
        @pl.when(i * n_tiles + t >= n_slots)
        def _():
            for hf in range(n_half):
                o_copy(0, 0, hf, slot).wait()
            if n_cast:
                copy_out(0, first_window, slot).wait()

        if n_cast:
            copy_in(0, first_window, slot).wait()
            wrap = t + 1 == n_tiles
            for_owner(jnp.where(wrap, i + 1, i), jnp.where(wrap, 0, t + 1),
                      lambda a, window: copy_in(a, window, (slot + 1) % n_slots).start(),
                      extra=jnp.logical_not(wrap & (i == n_grid - 1)))

    def consume(t, slot):
        for hf in range(n_half):
            o_buf[slot, hf] = _dot(h_ref[hf * half:(hf + 1) * half, :], w_buf[slot]).astype(o_buf.dtype)
        if n_cast:
            cout[slot] = cin[slot].astype(BF16)

    def after(t, slot):
        for hf in range(n_half):
            o_copy(i, t, hf, slot).start()
        if n_cast:
            for_owner(i, t, lambda a, window: copy_out(a, window, slot).start())

    _stream_weight_tiles(n_tiles, n_slots, w_copies, consume, before, after)

    @pl.when(i == n_grid - 1)
    def _():
        for slot in range(n_slots):
            for hf in range(n_half):
                o_copy(0, 0, hf, slot).wait()
            if n_cast:
                copy_out(0, first_window, slot).wait()


def _in_proj(x2, mod3, b0, seq, norm_g, w_bf, wg_bf, cast=(), half=512, tn=512):
    m, d = x2.shape
    n = min(w_bf.shape[1], PROJ_WIDTH)
    tm = INPROJ_HALVES * half
    assert n % tn == 0 and m % tm == 0 and seq % tm == 0
    n_grid = m // tm
    mod_spec = lambda k: pl.BlockSpec((1, 1, d), lambda i: ((b0 + (i * tm) // seq) * 6 + k, 0, 0))
    hbm = pl.BlockSpec(memory_space=pl.ANY)
    scratch = [pltpu.VMEM((INPROJ_HALVES, half, d), F32), pltpu.VMEM((tm, d), BF16),
               pltpu.VMEM((half, LANES), F32), pltpu.VMEM((2, SUBLANES, d), F32),
               pltpu.VMEM((INPROJ_SLOTS, d, tn), BF16), pltpu.VMEM((INPROJ_SLOTS, INPROJ_HALVES, half, tn), BF16),
               pltpu.SemaphoreType.DMA((INPROJ_HALVES,)), pltpu.SemaphoreType.DMA((INPROJ_SLOTS,)),
               pltpu.SemaphoreType.DMA((INPROJ_SLOTS, INPROJ_HALVES))]
    if cast:
        scratch += [pltpu.VMEM((INPROJ_SLOTS, CAST_ROWS, CAST_COLS), F32),
                    pltpu.VMEM((INPROJ_SLOTS, CAST_ROWS, CAST_COLS), BF16),
                    pltpu.SemaphoreType.DMA((INPROJ_SLOTS,)), pltpu.SemaphoreType.DMA((INPROJ_SLOTS,))]
    gates, proj, *converted = pl.pallas_call(
        functools.partial(_inproj_kernel, tn=tn, n_grid=n_grid, n_cast=len(cast)),
        grid=(n_grid,),
        in_specs=[mod_spec(0), mod_spec(1),
                  pl.BlockSpec((1, d), lambda i: (0, 0)),
                  pl.BlockSpec((d, GATE_PAD), lambda i: (0, 0)),
                  hbm, hbm] + [hbm] * len(cast),
        out_specs=[pl.BlockSpec((tm, GATE_PAD), lambda i: (i, 0)), hbm] + [hbm] * len(cast),
        out_shape=[jax.ShapeDtypeStruct((m, GATE_PAD), F32),
                   jax.ShapeDtypeStruct((m, n), BF16)]
                  + [jax.ShapeDtypeStruct(a.shape, BF16) for a in cast],
        scratch_shapes=scratch,
        compiler_params=_cparams(1),
        name="in_proj_cast" if cast else "in_proj",
    )(mod3, mod3, norm_g, wg_bf, x2, w_bf, *cast)
    return (proj, gates, *converted)


def _t5_bucket_table():
    half = N_BUCKETS // 2
    max_exact = half // 2
    qi = np.arange(BLOCK)[:, None]
    kj = np.arange(3 * BLOCK)[None, :]
    rel = kj - BLOCK - qi
    n = np.abs(rel)
    nf = np.maximum(n, 1).astype(np.float32)
    large = max_exact + (np.log(nf / max_exact) / math.log(MAX_DISTANCE / max_exact)
                         * (half - max_exact)).astype(np.int32)
    large = np.minimum(large, half - 1)
    return (np.where(rel > 0, half, 0) + np.where(n < max_exact, n, large)).astype(np.int32)


def _attn_kernel(rb_ref, sink_ref, bucket_ref, q_ref, k_ref, v_ref, o_ref, tbl_ref, *, nb):
    @pl.when((pl.program_id(0) == 0) & (pl.program_id(1) == 0))
    def _():
        qi = lax.broadcasted_iota(jnp.int32, (BLOCK, 3 * BLOCK), 0)
        kj = lax.broadcasted_iota(jnp.int32, (BLOCK, 3 * BLOCK), 1)
        in_band = jnp.abs(kj - BLOCK - qi) <= WINDOW
        bucket = bucket_ref[...]

        def per_head(h, carry):
            def per_bucket(bk, acc):
                return jnp.where(bucket == bk, rb_ref[bk, h], acc)
            acc = lax.fori_loop(0, N_BUCKETS, per_bucket, jnp.zeros((BLOCK, 3 * BLOCK), F32)) * LOG2E
            for v in range(4):
                keep = in_band
                if v & 1:
                    keep = keep & (kj >= BLOCK)
                if v & 2:
                    keep = keep & (kj < 2 * BLOCK)
                tbl_ref[v, h] = jnp.where(keep, acc, NEG_INF)
            return carry

        lax.fori_loop(0, N_ATTN_HEADS, per_head, 0)

    scale2 = HEAD_DIM ** -0.5 * LOG2E

    for sub in range(ATTN_QBLOCKS):
        n = ATTN_QBLOCKS * pl.program_id(1) + sub
        qrows = slice(sub * BLOCK, (sub + 1) * BLOCK)
        r_prev = pl.multiple_of(jnp.maximum(n - 1, 0) * BLOCK, BLOCK)
        r_cur = pl.multiple_of(n * BLOCK, BLOCK)
        r_next = pl.multiple_of(jnp.minimum(n + 1, nb - 1) * BLOCK, BLOCK)
        variant = (n == 0).astype(jnp.int32) + 2 * (n == nb - 1).astype(jnp.int32)
        for g in range(N_KV_HEADS):
            kc = slice(g * HEAD_DIM, (g + 1) * HEAD_DIM)
            k3 = jnp.concatenate([k_ref[0, pl.ds(r_prev, BLOCK), kc],
                                  k_ref[0, pl.ds(r_cur, BLOCK), kc],
                                  k_ref[0, pl.ds(r_next, BLOCK), kc]], axis=0)
            v3 = jnp.concatenate([v_ref[0, pl.ds(r_prev, BLOCK), kc],
                                  v_ref[0, pl.ds(r_cur, BLOCK), kc],
                                  v_ref[0, pl.ds(r_next, BLOCK), kc]], axis=0)
            heads = [g * GQA_GROUP + j for j in range(GQA_GROUP)]
            qs = jnp.concatenate([q_ref[0, qrows, h * HEAD_DIM:(h + 1) * HEAD_DIM] for h in heads], axis=0)
            s = _dot_nt(qs, k3)
            ps, inv_dens = [], []
            for j, h in enumerate(heads):
                sj = s[j * BLOCK:(j + 1) * BLOCK] * scale2 + tbl_ref[variant, h]
                sink = sink_ref[0, h] * LOG2E
                m = jnp.maximum(jnp.max(sj, axis=-1, keepdims=True), sink)
                p = jnp.exp2(sj - m)
                den = jnp.sum(p, axis=-1, keepdims=True) + jnp.exp2(sink - m)
                ps.append(p.astype(BF16))
                inv_dens.append(1.0 / den)
            o = _dot(jnp.concatenate(ps, axis=0), v3)
            for j, h in enumerate(heads):
                o_ref[0, qrows, h * HEAD_DIM:(h + 1) * HEAD_DIM] = (
                    o[j * BLOCK:(j + 1) * BLOCK] * inv_dens[j]).astype(o_ref.dtype)


def _attention(proj3, rel_bias, sink, bucket):
    bsz, seq, _ = proj3.shape
    nb = seq // BLOCK
    smem = pl.BlockSpec(memory_space=pltpu.SMEM)
    return pl.pallas_call(
        functools.partial(_attn_kernel, nb=nb),
        grid=(bsz, nb // ATTN_QBLOCKS),
        in_specs=[smem, smem,
                  pl.BlockSpec((BLOCK, 3 * BLOCK), lambda b, n: (0, 0)),
                  pl.BlockSpec((1, ATTN_QBLOCKS * BLOCK, ATTN_WIDTH), lambda b, n: (b, n, OFF_QA // ATTN_WIDTH)),
                  pl.BlockSpec((1, seq, KV_WIDTH), lambda b, n: (b, 0, OFF_KA // KV_WIDTH)),
                  pl.BlockSpec((1, seq, KV_WIDTH), lambda b, n: (b, 0, OFF_VA // KV_WIDTH))],
        out_specs=pl.BlockSpec((1, ATTN_QBLOCKS * BLOCK, ATTN_WIDTH), lambda b, n: (b, n, 0)),
        out_shape=jax.ShapeDtypeStruct((bsz, seq, ATTN_WIDTH), BF16),
        scratch_shapes=[pltpu.VMEM((4, N_ATTN_HEADS, BLOCK, 3 * BLOCK), F32)],
        compiler_params=_cparams(2),
        name="window_attn",
    )(rel_bias, sink, bucket, proj3, proj3, proj3)


def _gla_tri_mats():
    t = np.arange(GLA_BLOCK)
    same = (t[:, None] // GLA_CHUNK) == (t[None, :] // GLA_CHUNK)
    fwd = same & (t[None, :] <= t[:, None])
    bwd = same & (t[None, :] >= t[:, None])
    return np.stack([fwd, bwd]).astype(np.float32)


def _gla_kernel(tri_ref, msk_ref, q_ref, k_ref, v_ref, r_ref, gt_ref, wa_ref, ba_ref, ng_ref, o_ref,
                bcum_ref, qt_ref, stb_ref, st_ref, u_ref, acc_ref, *, seq):
    c = GLA_CHUNK
    cpb = GLA_BLOCK // c
    nblk = seq // GLA_BLOCK
    scale = GLA_DK ** -0.5
    anchor = (c // 2 - 1, c // 2)
    last = (c - 1, 0)

    def gates(d, bi):
        rows = pl.ds(pl.multiple_of(bi * GLA_BLOCK, GLA_BLOCK), GLA_BLOCK)
        z = _dot(gt_ref[0, rows, :].astype(BF16), wa_ref[d]) + ba_ref[d]
        lg = (jnp.minimum(z, 0.0) - jnp.log(1.0 + jnp.exp(-jnp.abs(z)))) * (1.0 / GATE_TEMP)
        hi = lg.astype(BF16)
        lo = (lg - hi.astype(F32)).astype(BF16)
        cum = _dot(tri_ref[d], jnp.concatenate([hi, lo], axis=1))
        bcum_ref[d, rows, :] = cum[:, :GLA_DK] + cum[:, GLA_DK:]

    st_ref[...] = jnp.zeros_like(st_ref)
    for sub in range(GLA_UNROLL):
        gates(0, sub)
        gates(1, nblk - 1 - sub)

    def scan_blocks(i, carry):
        for sub in range(GLA_UNROLL):
            ib = i * GLA_UNROLL + sub
            for d, bi in ((0, ib), (1, nblk - 1 - ib)):
                r0 = pl.multiple_of(bi * GLA_BLOCK, GLA_BLOCK)
                lanes = slice(d * GLA_DK, (d + 1) * GLA_DK)
                decays, qps, kps = [], [], []
                for ci in range(cpb):
                    rows = pl.ds(r0 + ci * c, c)
                    bc = bcum_ref[d, rows, :]
                    b_a = bc[anchor[d]:anchor[d] + 1]
                    b_l = bc[last[d]:last[d] + 1]
                    qpf = q_ref[0, rows, :].astype(F32) * scale * jnp.exp(bc - b_a)
                    kpf = k_ref[0, rows, :].astype(F32) * jnp.exp(b_a - bc)
                    qps.append(qpf.astype(BF16))
                    kps.append(kpf.astype(BF16))
                    qt_ref[rows, lanes] = (qpf * jnp.exp(b_a)).astype(BF16)
                    kd = (kpf * jnp.exp(b_l - b_a)).astype(BF16)
                    u_ref[d, sub * cpb + ci] = _dot_tn(v_ref[0, rows, :], kd)
                    decays.append(jnp.exp(b_l))
                rows = pl.ds(r0, GLA_BLOCK)
                s = _dot_nt(jnp.concatenate(qps, axis=0), jnp.concatenate(kps, axis=0))
                a = jnp.where(msk_ref[d] > 0.5, s, 0.0).astype(BF16)
                acc_ref[d, rows, :] = _dot(a, v_ref[0, rows, :])
                for s in range(GLA_DV // GLA_SLAB):
                    sl = slice(s * GLA_SLAB, (s + 1) * GLA_SLAB)
                    st = st_ref[d, sl, :]
                    for ci in (range(cpb) if d == 0 else reversed(range(cpb))):
                        stb_ref[bi * cpb + ci, sl, lanes] = st.astype(BF16)
                        st = st * decays[ci] + u_ref[d, sub * cpb + ci, sl, :]
                    st_ref[d, sl, :] = st
        for sub in range(GLA_UNROLL):
            ib = (i + 1) * GLA_UNROLL + sub
            gates(0, jnp.minimum(ib, nblk - 1))
            gates(1, jnp.maximum(nblk - 1 - ib, 0))
        return carry

    lax.fori_loop(0, nblk // GLA_UNROLL, scan_blocks, 0)

    ng = ng_ref[...]

    def post(r0):
        for ci in range(GLA_POST_BLOCKS * cpb):
            rows = pl.ds(r0 + ci * c, c)
            o = (acc_ref[0, rows, :] + acc_ref[1, rows, :]
                 + _dot_nt(qt_ref[rows, :], stb_ref[lax.div(r0, c) + ci]))
            ms = jnp.sum(o * o, axis=-1, keepdims=True) * (1.0 / GLA_DV)
            on = o * lax.rsqrt(ms + EPS) * ng
            o_ref[0, rows, :] = (on * _silu(r_ref[0, rows, :].astype(F32))).astype(o_ref.dtype)

    _row_loop(seq, GLA_POST_BLOCKS * GLA_BLOCK, post)


def _gla(proj3, gates3, tri, wa_pad, ba, norm_g):
    bsz, seq, _ = proj3.shape
    nc = seq // GLA_CHUNK
    assert seq % (GLA_UNROLL * GLA_BLOCK) == 0 and seq % (GLA_POST_BLOCKS * GLA_BLOCK) == 0
    whole = lambda shape: pl.BlockSpec(shape, lambda b, h: (0,) * len(shape))
    return pl.pallas_call(
        functools.partial(_gla_kernel, seq=seq),
        grid=(bsz, N_GLA_HEADS),
        in_specs=[whole((2, GLA_BLOCK, GLA_BLOCK)),
                  whole((2, GLA_BLOCK, GLA_BLOCK)),
                  pl.BlockSpec((1, seq, GLA_DK), lambda b, h: (b, 0, OFF_QG // GLA_DK + h)),
                  pl.BlockSpec((1, seq, GLA_DK), lambda b, h: (b, 0, OFF_KG // GLA_DK + h)),
                  pl.BlockSpec((1, seq, GLA_DV), lambda b, h: (b, 0, OFF_VG // GLA_DV + h)),
                  pl.BlockSpec((1, seq, GLA_DV), lambda b, h: (b, 0, OFF_RG // GLA_DV + h)),
                  pl.BlockSpec((1, seq, GATE_PAD), lambda b, h: (b, 0, 0)),
                  pl.BlockSpec((2, GATE_PAD, GLA_DK), lambda b, h: (0, 0, h)),
                  pl.BlockSpec((2, 1, GLA_DK), lambda b, h: (0, 0, h)),
                  whole((1, GLA_DV))],
        out_specs=pl.BlockSpec((1, seq, GLA_DV), lambda b, h: (b, 0, h)),
        out_shape=jax.ShapeDtypeStruct((bsz, seq, GLA_V_WIDTH), BF16),
        scratch_shapes=[pltpu.VMEM((2, seq, GLA_DK), F32),
                        pltpu.VMEM((seq, 2 * GLA_DK), BF16),
                        pltpu.VMEM((nc, GLA_DV, 2 * GLA_DK), BF16),
                        pltpu.VMEM((2, GLA_DV, GLA_DK), F32),
                        pltpu.VMEM((2, GLA_UNROLL * GLA_BLOCK // GLA_CHUNK, GLA_DV, GLA_DK), F32),
                        pltpu.VMEM((2, seq, GLA_DV), F32)],
        compiler_params=_cparams(2),
        name="gla",
    )(tri.astype(BF16), tri, proj3, proj3, proj3, proj3, gates3, wa_pad, ba, norm_g)


def _outproj_kernel(a_ref, g_ref, wa_ref, wg_ref, x_ref, g1_ref, o_ref):
    acc = _dot(a_ref[...], wa_ref[...]) + _dot(g_ref[...], wg_ref[...])
    o_ref[...] = x_ref[...] + g1_ref[0] * acc


def _out_proj(attn2, gla2, w_bf, x2, mod3, b0, seq, tm=1024, tn=1024):
    m, d = x2.shape
    ka = attn2.shape[1]
    kg = gla2.shape[1]
    return pl.pallas_call(
        _outproj_kernel,
        grid=(m // tm, d // tn),
        in_specs=[pl.BlockSpec((tm, ka), lambda i, j: (i, 0)),
                  pl.BlockSpec((tm, kg), lambda i, j: (i, 0)),
                  pl.BlockSpec((ka, tn), lambda i, j: (0, j)),
                  pl.BlockSpec((kg, tn), lambda i, j: (ka // kg, j)),
                  pl.BlockSpec((tm, tn), lambda i, j: (i, j)),
                  pl.BlockSpec((1, 1, tn), lambda i, j: ((b0 + (i * tm) // seq) * 6 + 2, 0, j))],
        out_specs=pl.BlockSpec((tm, tn), lambda i, j: (i, j)),
        out_shape=jax.ShapeDtypeStruct((m, d), F32),
        compiler_params=_cparams(2),
        name="out_proj",
    )(attn2, gla2, w_bf, w_bf, x2, mod3)


def _mlp_kernel(sh_ref, sc_ref, g2_ref, ng_ref, fg_ref, x_hbm, w1_hbm, w2_hbm, y_hbm,
                x_buf, acc, h_ref, r_ref, m_ref, w1_buf, w2_buf, sem_x, sem_w, sem_y, *, n_grid, tf, tn):
    n_half, half, d = x_buf.shape
    i = pl.program_id(0)

    def x_copy(step, hf):
        rows = pl.ds(pl.multiple_of((step * n_half + hf) * half, half), half)
        return pltpu.make_async_copy(x_hbm.at[rows, :], x_buf.at[hf], sem_x.at[hf])

    def y_copy(step, hf):
        rows = pl.ds(pl.multiple_of((step * n_half + hf) * half, half), half)
        return pltpu.make_async_copy(acc.at[hf], y_hbm.at[rows, :], sem_y.at[hf])

    def w_copies(f, slot):
        return (pltpu.make_async_copy(w1_hbm.at[:, pl.ds(f * tf, tf)], w1_buf.at[slot], sem_w.at[0, slot]),
                pltpu.make_async_copy(w2_hbm.at[pl.ds(f * tf, tf), :], w2_buf.at[slot], sem_w.at[1, slot]))

    @pl.when(i == 0)
    def _():
        for hf in range(n_half):
            x_copy(0, hf).start()
        acc[...] = jnp.zeros_like(acc)

    for hf in range(n_half):
        x_copy(i, hf).wait()
        _modulated_rms_rows(x_buf.at[hf], h_ref.at[pl.ds(hf * half, half)], r_ref, m_ref,
                            ng_ref[...], sc_ref[0], sh_ref[0], half)

        @pl.when(i > 0)
        def _(hf=hf):
            y_copy(i - 1, hf).wait()

    def consume(f, slot):
        for hf in range(n_half):
            u = _dot(h_ref[hf * half:(hf + 1) * half, :], w1_buf[slot])
            u = jnp.square(jnp.maximum(u, 0.0)).astype(BF16)
            for c in range(d // tn):
                cols = slice(c * tn, (c + 1) * tn)
                acc[hf, :, cols] = jnp.where(f == 0, 0.0, acc[hf, :, cols]) + _dot(u, w2_buf[slot, :, cols])

    _stream_weight_tiles(w1_hbm.shape[1] // tf, w1_buf.shape[0], w_copies, consume)

    m_ref[0] = jnp.broadcast_to(fg_ref[...], (SUBLANES, d))
    m_ref[1] = jnp.broadcast_to(g2_ref[0], (SUBLANES, d))
    for hf in range(n_half):
        x_half, o_half = x_buf.at[hf], acc.at[hf]

        def residual(rows, cols, x_half=x_half, o_half=o_half):
            x2 = x_half[rows, cols] + m_ref[1, :, cols] * o_half[rows, cols]
            o_half[rows, cols] = x2
            return x2

        _row_rsqrt(o_half, r_ref, half, load=residual)
        _scale_rows(o_half, o_half, r_ref, m_ref, half, shift=False)
        y_copy(i, hf).start()

        @pl.when(i + 1 < n_grid)
        def _(hf=hf):
            x_copy(i + 1, hf).start()

    @pl.when(i == n_grid - 1)
    def _():
        for hf in range(n_half):
            y_copy(i, hf).wait()


def _mlp(x1, mod3, b0, seq, norm_g, final_g, w1_bf, w2_bf, half=512, tf=512, tn=512):
    m, d = x1.shape
    tm = MLP_HALVES * half
    assert m % tm == 0 and seq % tm == 0
    n_grid = m // tm
    mod_spec = lambda k: pl.BlockSpec((1, 1, d), lambda i: ((b0 + (i * tm) // seq) * 6 + k, 0, 0))
    hbm = pl.BlockSpec(memory_space=pl.ANY)
    return pl.pallas_call(
        functools.partial(_mlp_kernel, n_grid=n_grid, tf=tf, tn=tn),
        grid=(n_grid,),
        in_specs=[mod_spec(3), mod_spec(4), mod_spec(5),
                  pl.BlockSpec((1, d), lambda i: (0, 0)),
                  pl.BlockSpec((1, d), lambda i: (0, 0)),
                  hbm, hbm, hbm],
        out_specs=hbm,
        out_shape=jax.ShapeDtypeStruct((m, d), F32),
        scratch_shapes=[pltpu.VMEM((MLP_HALVES, half, d), F32), pltpu.VMEM((MLP_HALVES, half, d), F32),
                        pltpu.VMEM((tm, d), BF16), pltpu.VMEM((half, LANES), F32),
                        pltpu.VMEM((2, SUBLANES, d), F32),
                        pltpu.VMEM((2, d, tf), BF16), pltpu.VMEM((2, tf, d), BF16),
                        pltpu.SemaphoreType.DMA((MLP_HALVES,)), pltpu.SemaphoreType.DMA((2, 2)),
                        pltpu.SemaphoreType.DMA((MLP_HALVES,))],
        compiler_params=_cparams(1),
        name="mlp",
    )(mod3, mod3, mod3, norm_g, final_g, x1, w1_bf, w2_bf)


LATE_WEIGHTS = ("w_mlp_in", "w_mlp_out", "w_out")


def _trunk_group(x, mod3, b0, p):
    bsz, seq, d = x.shape
    x2 = x.reshape(bsz * seq, d)
    pending = tuple(k for k in LATE_WEIGHTS if p[k].dtype != BF16)
    proj, gates, *converted = _in_proj(x2, mod3, b0, seq, p["norm1_g"], p["w_in"], p["w_gate"],
                                       cast=tuple(p[k] for k in pending))
    p = {**p, **dict(zip(pending, converted))}
    proj3 = proj.reshape(bsz, seq, PROJ_WIDTH)
    gates3 = gates.reshape(bsz, seq, GATE_PAD)
    attn = _attention(proj3, p["rel_bias"], p["sink"], p["bucket"])
    gla = _gla(proj3, gates3, p["tri"], p["wa_pad"], p["ba"], p["gla_norm_g"])
    x1 = _out_proj(attn.reshape(bsz * seq, ATTN_WIDTH), gla.reshape(bsz * seq, GLA_V_WIDTH),
                   p["w_out"], x2, mod3, b0, seq)
    y = _mlp(x1, mod3, b0, seq, p["norm2_g"], p["final_g"], p["w_mlp_in"], p["w_mlp_out"])
    return y.reshape(bsz, seq, d), p


def kernel(x_prompt, x_sample, c_prompt, c_sample, w_ada, b_ada, norm1_g, w_in, gla_wa_fwd, gla_ba_fwd, gla_wa_bwd, gla_ba_bwd, gla_norm_g, attn_sink, rel_bias, w_out, norm2_g, w_mlp_in, w_mlp_out, final_g):
    assert w_ada.shape[0] == 1, "single-layer trunk"
    d = x_prompt.shape[-1]
    bp, bs = c_prompt.shape[0], c_sample.shape[0]

    rows = -(-(bp + bs) // 16) * 16
    c_pad = jnp.zeros((rows, d), F32).at[:bp].set(c_prompt).at[bp:bp + bs].set(c_sample)
    mod = _ada(c_pad, w_ada[0], b_ada[0])
    mod3 = mod.reshape(rows * 6, 1, d)

    w_in0 = w_in[0]
    wa_pad = jnp.zeros((2, GATE_PAD, GLA_K_WIDTH), F32)
    wa_pad = wa_pad.at[0, :GATE_RANK].set(gla_wa_fwd[0]).at[1, GATE_RANK:2 * GATE_RANK].set(gla_wa_bwd[0])
    p = {
        "norm1_g": norm1_g[0].reshape(1, d),
        "norm2_g": norm2_g[0].reshape(1, d),
        "final_g": final_g.reshape(1, d),
        "w_in": w_in0.astype(BF16),
        "w_gate": jnp.pad(w_in0[:, OFF_GATE:], ((0, 0), (0, GATE_PAD - 2 * GATE_RANK))).astype(BF16),
        "w_out": w_out[0],
        "w_mlp_in": w_mlp_in[0],
        "w_mlp_out": w_mlp_out[0],
        "wa_pad": wa_pad.astype(BF16),
        "ba": jnp.stack([gla_ba_fwd[0], gla_ba_bwd[0]]).reshape(2, 1, GLA_K_WIDTH),
        "gla_norm_g": gla_norm_g[0].reshape(1, GLA_DV),
        "rel_bias": rel_bias,
        "sink": attn_sink[0].reshape(1, N_ATTN_HEADS),
        "bucket": jnp.asarray(_t5_bucket_table()),
        "tri": jnp.asarray(_gla_tri_mats()),
    }
    y_prompt, p = _trunk_group(x_prompt, mod3, 0, p)
    y_sample, _ = _trunk_group(x_sample, mod3, bp, p)
    return (y_prompt, y_sample)
```

```python
import functools
import math

import numpy as np
import jax
import jax.numpy as jnp
from jax import lax
from jax.experimental import pallas as pl
from jax.experimental.pallas import tpu as pltpu

F32 = jnp.float32
BF16 = jnp.bfloat16

HEAD_DIM = 128
N_ATTN_HEADS = 16
N_KV_HEADS = 4
GQA_GROUP = N_ATTN_HEADS // N_KV_HEADS
ATTN_WIDTH = N_ATTN_HEADS * HEAD_DIM
KV_WIDTH = N_KV_HEADS * HEAD_DIM
WINDOW = 128
BLOCK = 128
GLA_DV = 256
GLA_DK = 128
N_GLA_HEADS = 8
GLA_K_WIDTH = N_GLA_HEADS * GLA_DK
GLA_V_WIDTH = N_GLA_HEADS * GLA_DV
GATE_RANK = 16
GATE_TEMP = 16.0
N_BUCKETS = 32
MAX_DISTANCE = 128
EPS = 1e-6
NEG_INF = -1e30
LOG2E = math.log2(math.e)

OFF_QA = 0
OFF_KA = OFF_QA + ATTN_WIDTH
OFF_VA = OFF_KA + KV_WIDTH
OFF_QG = OFF_VA + KV_WIDTH
OFF_KG = OFF_QG + GLA_K_WIDTH
OFF_VG = OFF_KG + GLA_K_WIDTH
OFF_RG = OFF_VG + GLA_V_WIDTH
OFF_GATE = OFF_RG + GLA_V_WIDTH
PROJ_WIDTH = OFF_GATE
GATE_PAD = 128

GLA_CHUNK = 64
GLA_BLOCK = 256
GLA_SLAB = 64
GLA_UNROLL = 2
GLA_POST_BLOCKS = 4
INPROJ_SLOTS = 3
ATTN_QBLOCKS = 4
INPROJ_HALVES = 2
MLP_HALVES = 2
CAST_ROWS, CAST_COLS = 128, 4096
SUBLANES = 8
LANES = 128
NORM_ROWS = 16
RSQRT_GROUPS = 8

V7X_VMEM_BYTES = 64 * 1024 * 1024
VMEM_LIMIT = V7X_VMEM_BYTES * 15 // 16


def _cparams(n_axes, vmem=VMEM_LIMIT):
    return pltpu.CompilerParams(dimension_semantics=("arbitrary",) * n_axes,
                                vmem_limit_bytes=vmem)


def _dot(a, b):
    return jnp.dot(a, b, preferred_element_type=F32)


def _dot_nt(a, b):
    return lax.dot_general(a, b, (((1,), (1,)), ((), ())), preferred_element_type=F32)


def _dot_tn(a, b):
    return lax.dot_general(a, b, (((0,), (0,)), ((), ())), preferred_element_type=F32)


def _silu(x):
    return x * (1.0 / (1.0 + jnp.exp(-x)))


def _row_loop(n_rows, rows_per, body):
    def step(i, carry):
        body(pl.multiple_of(i * rows_per, rows_per))
        return carry
    lax.fori_loop(0, n_rows // rows_per, step, 0)


def _ada_kernel(c_ref, w_ref, b_ref, o_ref):
    a = _silu(c_ref[...]).astype(BF16)
    o_ref[...] = _dot(a, w_ref[...].astype(BF16)) + b_ref[...]


def _ada(c_pad, w_ada, b_ada, tn=512):
    rows, d = c_pad.shape
    n = w_ada.shape[1]
    return pl.pallas_call(
        _ada_kernel,
        grid=(n // tn,),
        in_specs=[pl.BlockSpec((rows, d), lambda j: (0, 0)),
                  pl.BlockSpec((d, tn), lambda j: (0, j)),
                  pl.BlockSpec((1, tn), lambda j: (0, j))],
        out_specs=pl.BlockSpec((rows, tn), lambda j: (0, j)),
        out_shape=jax.ShapeDtypeStruct((rows, n), F32),
        compiler_params=_cparams(1),
        name="ada",
    )(c_pad, w_ada, b_ada.reshape(1, n))


def _to_bf16_kernel(x_ref, o_ref):
    o_ref[...] = x_ref[...].astype(BF16)


def _to_bf16(w, rows=256):
    r, c = w.shape
    assert r % rows == 0
    return pl.pallas_call(
        _to_bf16_kernel,
        grid=(r // rows,),
        in_specs=[pl.BlockSpec((rows, c), lambda i: (i, 0))],
        out_specs=pl.BlockSpec((rows, c), lambda i: (i, 0)),
        out_shape=jax.ShapeDtypeStruct((r, c), BF16),
        compiler_params=_cparams(1),
        name="to_bf16",
    )(w)


def _row_rsqrt(x_ref, r_ref, n_rows, load=None):
    d = x_ref.shape[-1]

    def body(r0):
        for sub in range(RSQRT_GROUPS):
            rows = pl.ds(r0 + sub * SUBLANES, SUBLANES)
            acc = jnp.zeros((SUBLANES, LANES), F32)
            for j in range(d // LANES):
                cols = slice(j * LANES, (j + 1) * LANES)
                x = x_ref[rows, cols] if load is None else load(rows, cols)
                acc = acc + x * x
            r_ref[rows, :] = acc

    _row_loop(n_rows, RSQRT_GROUPS * SUBLANES, body)
    ms = jnp.sum(r_ref[...], axis=-1, keepdims=True) * (1.0 / d)
    r_ref[...] = jnp.broadcast_to(lax.rsqrt(ms + EPS), r_ref.shape)


def _scale_rows(x_ref, o_ref, r_ref, m_ref, n_rows, shift):
    d = x_ref.shape[-1]
    reps = NORM_ROWS // SUBLANES

    def body(r0):
        rows = pl.ds(r0, NORM_ROWS)
        r = r_ref[rows, :]
        for j in range(d // LANES):
            cols = slice(j * LANES, (j + 1) * LANES)
            y = x_ref[rows, cols] * r * jnp.concatenate([m_ref[0, :, cols]] * reps, axis=0)
            if shift:
                y = y + jnp.concatenate([m_ref[1, :, cols]] * reps, axis=0)
            o_ref[rows, cols] = y.astype(o_ref.dtype)

    _row_loop(n_rows, NORM_ROWS, body)


def _modulated_rms_rows(x_ref, h_ref, r_ref, m_ref, g, sc, sh, n_rows):
    d = x_ref.shape[-1]
    m_ref[0] = jnp.broadcast_to(g * (1.0 + sc), (SUBLANES, d))
    m_ref[1] = jnp.broadcast_to(sh, (SUBLANES, d))
    _row_rsqrt(x_ref, r_ref, n_rows)
    _scale_rows(x_ref, h_ref, r_ref, m_ref, n_rows, shift=True)


def _stream_weight_tiles(n_tiles, n_slots, copies, consume, before=None, after=None):
    assert n_tiles % n_slots == 0, "one loop iteration consumes one tile per slot"
    ahead = n_slots - 1

    @pl.when(pl.program_id(0) == 0)
    def _():
        for t in range(ahead):
            for cp in copies(t, t):
                cp.start()

    def ring(k, carry):
        for slot in range(n_slots):
            t = n_slots * k + slot
            for cp in copies(t, slot):
                cp.wait()
            t_next = jnp.where(t + ahead >= n_tiles, t + ahead - n_tiles, t + ahead)
            for cp in copies(t_next, (slot + ahead) % n_slots):
                cp.start()
            if before is not None:
                before(t, slot)
            consume(t, slot)
            if after is not None:
                after(t, slot)
        return carry

    lax.fori_loop(0, n_tiles // n_slots, ring, 0)

    @pl.when(pl.program_id(0) == pl.num_programs(0) - 1)
    def _():
        for t in range(ahead):
            for cp in copies(t, t):
                cp.wait()


def _cast_chunk_plan(shapes, n_grid, n_tiles):
    plan, lo = [], 0
    for rows, cols in shapes:
        assert rows % (n_grid * CAST_ROWS) == 0 and cols % CAST_COLS == 0
        rows_pg = rows // n_grid
        row_chunks = rows_pg // CAST_ROWS
        hi = lo + row_chunks * (cols // CAST_COLS)
        plan.append((lo, hi, row_chunks, rows_pg))
        lo = hi
    assert lo == n_tiles, "every (grid step, ring tile) pair converts exactly one chunk"
    return plan


def _inproj_kernel(*refs, tn, n_grid, n_cast):
    sh_ref, sc_ref, g_ref, wg_ref, x_hbm, w_hbm = refs[:6]
    srcs = refs[6:6 + n_cast]
    og_ref, o_hbm = refs[6 + n_cast:8 + n_cast]
    dsts = refs[8 + n_cast:8 + 2 * n_cast]
    x_buf, h_ref, r_ref, m_ref, w_buf, o_buf, sem_x, sem_w, sem_o = refs[8 + 2 * n_cast:17 + 2 * n_cast]
    n_half, half, _ = x_buf.shape
    n_tiles = o_hbm.shape[1] // tn
    n_slots = w_buf.shape[0]
    i = pl.program_id(0)

    def x_copy(step, hf):
        rows = pl.ds(pl.multiple_of((step * n_half + hf) * half, half), half)
        return pltpu.make_async_copy(x_hbm.at[rows, :], x_buf.at[hf], sem_x.at[hf])

    def w_copies(t, slot):
        return (pltpu.make_async_copy(w_hbm.at[:, pl.ds(t * tn, tn)], w_buf.at[slot], sem_w.at[slot]),)

    def o_copy(step, t, hf, slot):
        rows = pl.ds(pl.multiple_of((step * n_half + hf) * half, half), half)
        cols = pl.ds(pl.multiple_of(t * tn, tn), tn)
        return pltpu.make_async_copy(o_buf.at[slot, hf], o_hbm.at[rows, cols], sem_o.at[slot, hf])

    if n_cast:
        cin, cout, sem_in, sem_out = refs[17 + 2 * n_cast:]
        plan = _cast_chunk_plan([s.shape for s in srcs], n_grid, n_tiles)
        first_window = (pl.ds(0, CAST_ROWS), pl.ds(0, CAST_COLS))

        def for_owner(step, t, fn, extra=True):
            for a, (lo, hi, row_chunks, rows_pg) in enumerate(plan):
                @pl.when((t >= lo) & (t < hi) & extra)
                def _(a=a, lo=lo, row_chunks=row_chunks, rows_pg=rows_pg):
                    j = t - lo
                    row = step * rows_pg + (j % row_chunks) * CAST_ROWS
                    col = (j // row_chunks) * CAST_COLS
                    fn(a, (pl.ds(pl.multiple_of(row, CAST_ROWS), CAST_ROWS),
                           pl.ds(pl.multiple_of(col, CAST_COLS), CAST_COLS)))

        def copy_in(a, window, slot):
            return pltpu.make_async_copy(srcs[a].at[window], cin.at[slot], sem_in.at[slot])

        def copy_out(a, window, slot):
            return pltpu.make_async_copy(cout.at[slot], dsts[a].at[window], sem_out.at[slot])

    @pl.when(i == 0)
    def _():
        for hf in range(n_half):
            x_copy(0, hf).start()
        if n_cast:
            copy_in(0, first_window, 0).start()

    for hf in range(n_half):
        x_copy(i, hf).wait()
        _modulated_rms_rows(x_buf.at[hf], h_ref.at[pl.ds(hf * half, half)], r_ref, m_ref,
                            g_ref[...], sc_ref[0], sh_ref[0], half)

        @pl.when(i + 1 < n_grid)
        def _(hf=hf):
            x_copy(i + 1, hf).start()

    og_ref[...] = _dot(h_ref[...], wg_ref[...])

    def before(t, slot):
        @pl.when(i * n_tiles + t >= n_slots)
        def _():
            for hf in range(n_half):
                o_copy(0, 0, hf, slot).wait()
            if n_cast:
                copy_out(0, first_window, slot).wait()

        if n_cast:
            copy_in(0, first_window, slot).wait()
            wrap = t + 1 == n_tiles
            for_owner(jnp.where(wrap, i + 1, i), jnp.where(wrap, 0, t + 1),
                      lambda a, window: copy_in(a, window, (slot + 1) % n_slots).start(),
                      extra=jnp.logical_not(wrap & (i == n_grid - 1)))

    def consume(t, slot):
        for hf in range(n_half):
            o_buf[slot, hf] = _dot(h_ref[hf * half:(hf + 1) * half, :], w_buf[slot]).astype(o_buf.dtype)
        if n_cast:
            cout[slot] = cin[slot].astype(BF16)

    def after(t, slot):
        for hf in range(n_half):
            o_copy(i, t, hf, slot).start()
        if n_cast:
            for_owner(i, t, lambda a, window: copy_out(a, window, slot).start())

    _stream_weight_tiles(n_tiles, n_slots, w_copies, consume, before, after)

    @pl.when(i == n_grid - 1)
    def _():
        for slot in range(n_slots):
            for hf in range(n_half):
                o_copy(0, 0, hf, slot).wait()
            if n_cast:
                copy_out(0, first_window, slot).wait()


def _in_proj(x2, mod3, b0, seq, norm_g, w_bf, wg_bf, cast=(), half=512, tn=512):
    m, d = x2.shape
    n = min(w_bf.shape[1], PROJ_WIDTH)
    tm = INPROJ_HALVES * half
    assert n % tn == 0 and m % tm == 0 and seq % tm == 0
    n_grid = m // tm
    mod_spec = lambda k: pl.BlockSpec((1, 1, d), lambda i: ((b0 + (i * tm) // seq) * 6 + k, 0, 0))
    hbm = pl.BlockSpec(memory_space=pl.ANY)
    scratch = [pltpu.VMEM((INPROJ_HALVES, half, d), F32), pltpu.VMEM((tm, d), BF16),
               pltpu.VMEM((half, LANES), F32), pltpu.VMEM((2, SUBLANES, d), F32),
               pltpu.VMEM((INPROJ_SLOTS, d, tn), BF16), pltpu.VMEM((INPROJ_SLOTS, INPROJ_HALVES, half, tn), BF16),
               pltpu.SemaphoreType.DMA((INPROJ_HALVES,)), pltpu.SemaphoreType.DMA((INPROJ_SLOTS,)),
               pltpu.SemaphoreType.DMA((INPROJ_SLOTS, INPROJ_HALVES))]
    if cast:
        scratch += [pltpu.VMEM((INPROJ_SLOTS, CAST_ROWS, CAST_COLS), F32),
                    pltpu.VMEM((INPROJ_SLOTS, CAST_ROWS, CAST_COLS), BF16),
                    pltpu.SemaphoreType.DMA((INPROJ_SLOTS,)), pltpu.SemaphoreType.DMA((INPROJ_SLOTS,))]
    gates, proj, *converted = pl.pallas_call(
        functools.partial(_inproj_kernel, tn=tn, n_grid=n_grid, n_cast=len(cast)),
        grid=(n_grid,),
        in_specs=[mod_spec(0), mod_spec(1),
                  pl.BlockSpec((1, d), lambda i: (0, 0)),
                  pl.BlockSpec((d, GATE_PAD), lambda i: (0, 0)),
                  hbm, hbm] + [hbm] * len(cast),
        out_specs=[pl.BlockSpec((tm, GATE_PAD), lambda i: (i, 0)), hbm] + [hbm] * len(cast),
        out_shape=[jax.ShapeDtypeStruct((m, GATE_PAD), F32),
                   jax.ShapeDtypeStruct((m, n), BF16)]
                  + [jax.ShapeDtypeStruct(a.shape, BF16) for a in cast],
        scratch_shapes=scratch,
        compiler_params=_cparams(1),
        name="in_proj_cast" if cast else "in_proj",
    )(mod3, mod3, norm_g, wg_bf, x2, w_bf, *cast)
    return (proj, gates, *converted)


def _t5_bucket_table():
    half = N_BUCKETS // 2
    max_exact = half // 2
    qi = np.arange(BLOCK)[:, None]
    kj = np.arange(3 * BLOCK)[None, :]
    rel = kj - BLOCK - qi
    n = np.abs(rel)
    nf = np.maximum(n, 1).astype(np.float32)
    large = max_exact + (np.log(nf / max_exact) / math.log(MAX_DISTANCE / max_exact)
                         * (half - max_exact)).astype(np.int32)
    large = np.minimum(large, half - 1)
    return (np.where(rel > 0, half, 0) + np.where(n < max_exact, n, large)).astype(np.int32)


def _attn_kernel(rb_ref, sink_ref, bucket_ref, q_ref, k_ref, v_ref, o_ref, tbl_ref, *, nb):
    @pl.when((pl.program_id(0) == 0) & (pl.program_id(1) == 0))
    def _():
        qi = lax.broadcasted_iota(jnp.int32, (BLOCK, 3 * BLOCK), 0)
        kj = lax.broadcasted_iota(jnp.int32, (BLOCK, 3 * BLOCK), 1)
        in_band = jnp.abs(kj - BLOCK - qi) <= WINDOW
        bucket = bucket_ref[...]

        def per_head(h, carry):
            def per_bucket(bk, acc):
                return jnp.where(bucket == bk, rb_ref[bk, h], acc)
            acc = lax.fori_loop(0, N_BUCKETS, per_bucket, jnp.zeros((BLOCK, 3 * BLOCK), F32)) * LOG2E
            for v in range(4):
                keep = in_band
                if v & 1:
                    keep = keep & (kj >= BLOCK)
                if v & 2:
                    keep = keep & (kj < 2 * BLOCK)
                tbl_ref[v, h] = jnp.where(keep, acc, NEG_INF)
            return carry

        lax.fori_loop(0, N_ATTN_HEADS, per_head, 0)

    scale2 = HEAD_DIM ** -0.5 * LOG2E

    for sub in range(ATTN_QBLOCKS):
        n = ATTN_QBLOCKS * pl.program_id(1) + sub
        qrows = slice(sub * BLOCK, (sub + 1) * BLOCK)
        r_prev = pl.multiple_of(jnp.maximum(n - 1, 0) * BLOCK, BLOCK)
        r_cur = pl.multiple_of(n * BLOCK, BLOCK)
        r_next = pl.multiple_of(jnp.minimum(n + 1, nb - 1) * BLOCK, BLOCK)
        variant = (n == 0).astype(jnp.int32) + 2 * (n == nb - 1).astype(jnp.int32)
        for g in range(N_KV_HEADS):
            kc = slice(g * HEAD_DIM, (g + 1) * HEAD_DIM)
            k3 = jnp.concatenate([k_ref[0, pl.ds(r_prev, BLOCK), kc],
                                  k_ref[0, pl.ds(r_cur, BLOCK), kc],
                                  k_ref[0, pl.ds(r_next, BLOCK), kc]], axis=0)
            v3 = jnp.concatenate([v_ref[0, pl.ds(r_prev, BLOCK), kc],
                                  v_ref[0, pl.ds(r_cur, BLOCK), kc],
                                  v_ref[0, pl.ds(r_next, BLOCK), kc]], axis=0)
            heads = [g * GQA_GROUP + j for j in range(GQA_GROUP)]
            qs = jnp.concatenate([q_ref[0, qrows, h * HEAD_DIM:(h + 1) * HEAD_DIM] for h in heads], axis=0)
            s = _dot_nt(qs, k3)
            ps, inv_dens = [], []
            for j, h in enumerate(heads):
                sj = s[j * BLOCK:(j + 1) * BLOCK] * scale2 + tbl_ref[variant, h]
                sink = sink_ref[0, h] * LOG2E
                m = jnp.maximum(jnp.max(sj, axis=-1, keepdims=True), sink)
                p = jnp.exp2(sj - m)
                den = jnp.sum(p, axis=-1, keepdims=True) + jnp.exp2(sink - m)
                ps.append(p.astype(BF16))
                inv_dens.append(1.0 / den)
            o = _dot(jnp.concatenate(ps, axis=0), v3)
            for j, h in enumerate(heads):
                o_ref[0, qrows, h * HEAD_DIM:(h + 1) * HEAD_DIM] = (
                    o[j * BLOCK:(j + 1) * BLOCK] * inv_dens[j]).astype(o_ref.dtype)


def _attention(proj3, rel_bias, sink, bucket):
    bsz, seq, _ = proj3.shape
    nb = seq // BLOCK
    smem = pl.BlockSpec(memory_space=pltpu.SMEM)
    return pl.pallas_call(
        functools.partial(_attn_kernel, nb=nb),
        grid=(bsz, nb // ATTN_QBLOCKS),
        in_specs=[smem, smem,
                  pl.BlockSpec((BLOCK, 3 * BLOCK), lambda b, n: (0, 0)),
                  pl.BlockSpec((1, ATTN_QBLOCKS * BLOCK, ATTN_WIDTH), lambda b, n: (b, n, OFF_QA // ATTN_WIDTH)),
                  pl.BlockSpec((1, seq, KV_WIDTH), lambda b, n: (b, 0, OFF_KA // KV_WIDTH)),
                  pl.BlockSpec((1, seq, KV_WIDTH), lambda b, n: (b, 0, OFF_VA // KV_WIDTH))],
        out_specs=pl.BlockSpec((1, ATTN_QBLOCKS * BLOCK, ATTN_WIDTH), lambda b, n: (b, n, 0)),
        out_shape=jax.ShapeDtypeStruct((bsz, seq, ATTN_WIDTH), BF16),
        scratch_shapes=[pltpu.VMEM((4, N_ATTN_HEADS, BLOCK, 3 * BLOCK), F32)],
        compiler_params=_cparams(2),
        name="window_attn",
    )(rel_bias, sink, bucket, proj3, proj3, proj3)


def _gla_tri_mats():
    t = np.arange(GLA_BLOCK)
    same = (t[:, None] // GLA_CHUNK) == (t[None, :] // GLA_CHUNK)
    fwd = same & (t[None, :] <= t[:, None])
    bwd = same & (t[None, :] >= t[:, None])
    return np.stack([fwd, bwd]).astype(np.float32)


def _gla_kernel(tri_ref, msk_ref, q_ref, k_ref, v_ref, r_ref, gt_ref, wa_ref, ba_ref, ng_ref, o_ref,
                bcum_ref, qt_ref, stb_ref, st_ref, u_ref, acc_ref, *, seq):
    c = GLA_CHUNK
    cpb = GLA_BLOCK // c
    nblk = seq // GLA_BLOCK
    scale = GLA_DK ** -0.5
    anchor = (c // 2 - 1, c // 2)
    last = (c - 1, 0)

    def gates(d, bi):
        rows = pl.ds(pl.multiple_of(bi * GLA_BLOCK, GLA_BLOCK), GLA_BLOCK)
        z = _dot(gt_ref[0, rows, :].astype(BF16), wa_ref[d]) + ba_ref[d]
        lg = (jnp.minimum(z, 0.0) - jnp.log(1.0 + jnp.exp(-jnp.abs(z)))) * (1.0 / GATE_TEMP)
        hi = lg.astype(BF16)
        lo = (lg - hi.astype(F32)).astype(BF16)
        cum = _dot(tri_ref[d], jnp.concatenate([hi, lo], axis=1))
        bcum_ref[d, rows, :] = cum[:, :GLA_DK] + cum[:, GLA_DK:]

    st_ref[...] = jnp.zeros_like(st_ref)
    for sub in range(GLA_UNROLL):
        gates(0, sub)
        gates(1, nblk - 1 - sub)

    def scan_blocks(i, carry):
        for sub in range(GLA_UNROLL):
            ib = i * GLA_UNROLL + sub
            for d, bi in ((0, ib), (1, nblk - 1 - ib)):
                r0 = pl.multiple_of(bi * GLA_BLOCK, GLA_BLOCK)
                lanes = slice(d * GLA_DK, (d + 1) * GLA_DK)
                decays, qps, kps = [], [], []
                for ci in range(cpb):
                    rows = pl.ds(r0 + ci * c, c)
                    bc = bcum_ref[d, rows, :]
                    b_a = bc[anchor[d]:anchor[d] + 1]
                    b_l = bc[last[d]:last[d] + 1]
                    qpf = q_ref[0, rows, :].astype(F32) * scale * jnp.exp(bc - b_a)
                    kpf = k_ref[0, rows, :].astype(F32) * jnp.exp(b_a - bc)
                    qps.append(qpf.astype(BF16))
                    kps.append(kpf.astype(BF16))
                    qt_ref[rows, lanes] = (qpf * jnp.exp(b_a)).astype(BF16)
                    kd = (kpf * jnp.exp(b_l - b_a)).astype(BF16)
                    u_ref[d, sub * cpb + ci] = _dot_tn(v_ref[0, rows, :], kd)
                    decays.append(jnp.exp(b_l))
                rows = pl.ds(r0, GLA_BLOCK)
                s = _dot_nt(jnp.concatenate(qps, axis=0), jnp.concatenate(kps, axis=0))
                a = jnp.where(msk_ref[d] > 0.5, s, 0.0).astype(BF16)
                acc_ref[d, rows, :] = _dot(a, v_ref[0, rows, :])
                for s in range(GLA_DV // GLA_SLAB):
                    sl = slice(s * GLA_SLAB, (s + 1) * GLA_SLAB)
                    st = st_ref[d, sl, :]
                    for ci in (range(cpb) if d == 0 else reversed(range(cpb))):
                        stb_ref[bi * cpb + ci, sl, lanes] = st.astype(BF16)
                        st = st * decays[ci] + u_ref[d, sub * cpb + ci, sl, :]
                    st_ref[d, sl, :] = st
        for sub in range(GLA_UNROLL):
            ib = (i + 1) * GLA_UNROLL + sub
            gates(0, jnp.minimum(ib, nblk - 1))
            gates(1, jnp.maximum(nblk - 1 - ib, 0))
        return carry

    lax.fori_loop(0, nblk // GLA_UNROLL, scan_blocks, 0)

    ng = ng_ref[...]

    def post(r0):
        for ci in range(GLA_POST_BLOCKS * cpb):
            rows = pl.ds(r0 + ci * c, c)
            o = (acc_ref[0, rows, :] + acc_ref[1, rows, :]
                 + _dot_nt(qt_ref[rows, :], stb_ref[lax.div(r0, c) + ci]))
            ms = jnp.sum(o * o, axis=-1, keepdims=True) * (1.0 / GLA_DV)
            on = o * lax.rsqrt(ms + EPS) * ng
            o_ref[0, rows, :] = (on * _silu(r_ref[0, rows, :].astype(F32))).astype(o_ref.dtype)

    _row_loop(seq, GLA_POST_BLOCKS * GLA_BLOCK, post)


def _gla(proj3, gates3, tri, wa_pad, ba, norm_g):
    bsz, seq, _ = proj3.shape
    nc = seq // GLA_CHUNK
    assert seq % (GLA_UNROLL * GLA_BLOCK) == 0 and seq % (GLA_POST_BLOCKS * GLA_BLOCK) == 0
    whole = lambda shape: pl.BlockSpec(shape, lambda b, h: (0,) * len(shape))
    return pl.pallas_call(
        functools.partial(_gla_kernel, seq=seq),
        grid=(bsz, N_GLA_HEADS),
        in_specs=[whole((2, GLA_BLOCK, GLA_BLOCK)),
                  whole((2, GLA_BLOCK, GLA_BLOCK)),
                  pl.BlockSpec((1, seq, GLA_DK), lambda b, h: (b, 0, OFF_QG // GLA_DK + h)),
                  pl.BlockSpec((1, seq, GLA_DK), lambda b, h: (b, 0, OFF_KG // GLA_DK + h)),
                  pl.BlockSpec((1, seq, GLA_DV), lambda b, h: (b, 0, OFF_VG // GLA_DV + h)),
                  pl.BlockSpec((1, seq, GLA_DV), lambda b, h: (b, 0, OFF_RG // GLA_DV + h)),
                  pl.BlockSpec((1, seq, GATE_PAD), lambda b, h: (b, 0, 0)),
                  pl.BlockSpec((2, GATE_PAD, GLA_DK), lambda b, h: (0, 0, h)),
                  pl.BlockSpec((2, 1, GLA_DK), lambda b, h: (0, 0, h)),
                  whole((1, GLA_DV))],
        out_specs=pl.BlockSpec((1, seq, GLA_DV), lambda b, h: (b, 0, h)),
        out_shape=jax.ShapeDtypeStruct((bsz, seq, GLA_V_WIDTH), BF16),
        scratch_shapes=[pltpu.VMEM((2, seq, GLA_DK), F32),
                        pltpu.VMEM((seq, 2 * GLA_DK), BF16),
                        pltpu.VMEM((nc, GLA_DV, 2 * GLA_DK), BF16),
                        pltpu.VMEM((2, GLA_DV, GLA_DK), F32),
                        pltpu.VMEM((2, GLA_UNROLL * GLA_BLOCK // GLA_CHUNK, GLA_DV, GLA_DK), F32),
                        pltpu.VMEM((2, seq, GLA_DV), F32)],
        compiler_params=_cparams(2),
        name="gla",
    )(tri.astype(BF16), tri, proj3, proj3, proj3, proj3, gates3, wa_pad, ba, norm_g)


def _outproj_kernel(a_ref, g_ref, wa_ref, wg_ref, x_ref, g1_ref, o_ref):
    acc = _dot(a_ref[...], wa_ref[...]) + _dot(g_ref[...], wg_ref[...])
    o_ref[...] = x_ref[...] + g1_ref[0] * acc


def _out_proj(attn2, gla2, w_bf, x2, mod3, b0, seq, tm=1024, tn=1024):
    m, d = x2.shape
    ka = attn2.shape[1]
    kg = gla2.shape[1]
    return pl.pallas_call(
        _outproj_kernel,
        grid=(m // tm, d // tn),
        in_specs=[pl.BlockSpec((tm, ka), lambda i, j: (i, 0)),
                  pl.BlockSpec((tm, kg), lambda i, j: (i, 0)),
                  pl.BlockSpec((ka, tn), lambda i, j: (0, j)),
                  pl.BlockSpec((kg, tn), lambda i, j: (ka // kg, j)),
                  pl.BlockSpec((tm, tn), lambda i, j: (i, j)),
                  pl.BlockSpec((1, 1, tn), lambda i, j: ((b0 + (i * tm) // seq) * 6 + 2, 0, j))],
        out_specs=pl.BlockSpec((tm, tn), lambda i, j: (i, j)),
        out_shape=jax.ShapeDtypeStruct((m, d), F32),
        compiler_params=_cparams(2),
        name="out_proj",
    )(attn2, gla2, w_bf, w_bf, x2, mod3)


def _mlp_kernel(sh_ref, sc_ref, g2_ref, ng_ref, fg_ref, x_hbm, w1_hbm, w2_hbm, y_hbm,
                x_buf, acc, h_ref, r_ref, m_ref, w1_buf, w2_buf, sem_x, sem_w, sem_y, *, n_grid, tf, tn):
    n_half, half, d = x_buf.shape
    i = pl.program_id(0)

    def x_copy(step, hf):
        rows = pl.ds(pl.multiple_of((step * n_half + hf) * half, half), half)
        return pltpu.make_async_copy(x_hbm.at[rows, :], x_buf.at[hf], sem_x.at[hf])

    def y_copy(step, hf):
        rows = pl.ds(pl.multiple_of((step * n_half + hf) * half, half), half)
        return pltpu.make_async_copy(acc.at[hf], y_hbm.at[rows, :], sem_y.at[hf])

    def w_copies(f, slot):
        return (pltpu.make_async_copy(w1_hbm.at[:, pl.ds(f * tf, tf)], w1_buf.at[slot], sem_w.at[0, slot]),
                pltpu.make_async_copy(w2_hbm.at[pl.ds(f * tf, tf), :], w2_buf.at[slot], sem_w.at[1, slot]))

    @pl.when(i == 0)
    def _():
        for hf in range(n_half):
            x_copy(0, hf).start()
        acc[...] = jnp.zeros_like(acc)

    for hf in range(n_half):
        x_copy(i, hf).wait()
        _modulated_rms_rows(x_buf.at[hf], h_ref.at[pl.ds(hf * half, half)], r_ref, m_ref,
                            ng_ref[...], sc_ref[0], sh_ref[0], half)

        @pl.when(i > 0)
        def _(hf=hf):
            y_copy(i - 1, hf).wait()

    def consume(f, slot):
        for hf in range(n_half):
            u = _dot(h_ref[hf * half:(hf + 1) * half, :], w1_buf[slot])
            u = jnp.square(jnp.maximum(u, 0.0)).astype(BF16)
            for c in range(d // tn):
                cols = slice(c * tn, (c + 1) * tn)
                acc[hf, :, cols] = jnp.where(f == 0, 0.0, acc[hf, :, cols]) + _dot(u, w2_buf[slot, :, cols])

    _stream_weight_tiles(w1_hbm.shape[1] // tf, w1_buf.shape[0], w_copies, consume)

    m_ref[0] = jnp.broadcast_to(fg_ref[...], (SUBLANES, d))
    m_ref[1] = jnp.broadcast_to(g2_ref[0], (SUBLANES, d))
    for hf in range(n_half):
        x_half, o_half = x_buf.at[hf], acc.at[hf]

        def residual(rows, cols, x_half=x_half, o_half=o_half):
            x2 = x_half[rows, cols] + m_ref[1, :, cols] * o_half[rows, cols]
            o_half[rows, cols] = x2
            return x2

        _row_rsqrt(o_half, r_ref, half, load=residual)
        _scale_rows(o_half, o_half, r_ref, m_ref, half, shift=False)
        y_copy(i, hf).start()

        @pl.when(i + 1 < n_grid)
        def _(hf=hf):
            x_copy(i + 1, hf).start()

    @pl.when(i == n_grid - 1)
    def _():
        for hf in range(n_half):
            y_copy(i, hf).wait()


def _mlp(x1, mod3, b0, seq, norm_g, final_g, w1_bf, w2_bf, half=512, tf=512, tn=512):
    m, d = x1.shape
    tm = MLP_HALVES * half
    assert m % tm == 0 and seq % tm == 0
    n_grid = m // tm
    mod_spec = lambda k: pl.BlockSpec((1, 1, d), lambda i: ((b0 + (i * tm) // seq) * 6 + k, 0, 0))
    hbm = pl.BlockSpec(memory_space=pl.ANY)
    return pl.pallas_call(
        functools.partial(_mlp_kernel, n_grid=n_grid, tf=tf, tn=tn),
        grid=(n_grid,),
        in_specs=[mod_spec(3), mod_spec(4), mod_spec(5),
                  pl.BlockSpec((1, d), lambda i: (0, 0)),
                  pl.BlockSpec((1, d), lambda i: (0, 0)),
                  hbm, hbm, hbm],
        out_specs=hbm,
        out_shape=jax.ShapeDtypeStruct((m, d), F32),
        scratch_shapes=[pltpu.VMEM((MLP_HALVES, half, d), F32), pltpu.VMEM((MLP_HALVES, half, d), F32),
                        pltpu.VMEM((tm, d), BF16), pltpu.VMEM((half, LANES), F32),
                        pltpu.VMEM((2, SUBLANES, d), F32),
                        pltpu.VMEM((2, d, tf), BF16), pltpu.VMEM((2, tf, d), BF16),
                        pltpu.SemaphoreType.DMA((MLP_HALVES,)), pltpu.SemaphoreType.DMA((2, 2)),
                        pltpu.SemaphoreType.DMA((MLP_HALVES,))],
        compiler_params=_cparams(1),
        name="mlp",
    )(mod3, mod3, mod3, norm_g, final_g, x1, w1_bf, w2_bf)


LATE_WEIGHTS = ("w_mlp_in", "w_mlp_out", "w_out")


def _trunk_group(x, mod3, b0, p):
    bsz, seq, d = x.shape
    x2 = x.reshape(bsz * seq, d)
    pending = tuple(k for k in LATE_WEIGHTS if p[k].dtype != BF16)
    proj, gates, *converted = _in_proj(x2, mod3, b0, seq, p["norm1_g"], p["w_in"], p["w_gate"],
                                       cast=tuple(p[k] for k in pending))
    p = {**p, **dict(zip(pending, converted))}
    proj3 = proj.reshape(bsz, seq, PROJ_WIDTH)
    gates3 = gates.reshape(bsz, seq, GATE_PAD)
    attn = _attention(proj3, p["rel_bias"], p["sink"], p["bucket"])
    gla = _gla(proj3, gates3, p["tri"], p["wa_pad"], p["ba"], p["gla_norm_g"])
    x1 = _out_proj(attn.reshape(bsz * seq, ATTN_WIDTH), gla.reshape(bsz * seq, GLA_V_WIDTH),
                   p["w_out"], x2, mod3, b0, seq)
    y = _mlp(x1, mod3, b0, seq, p["norm2_g"], p["final_g"], p["w_mlp_in"], p["w_mlp_out"])
    return y.reshape(bsz, seq, d), p


def kernel(x_prompt, x_sample, c_prompt, c_sample, w_ada, b_ada, norm1_g, w_in, gla_wa_fwd, gla_ba_fwd, gla_wa_bwd, gla_ba_bwd, gla_norm_g, attn_sink, rel_bias, w_out, norm2_g, w_mlp_in, w_mlp_out, final_g):
    assert w_ada.shape[0] == 1, "single-layer trunk"
    d = x_prompt.shape[-1]
    bp, bs = c_prompt.shape[0], c_sample.shape[0]

    rows = -(-(bp + bs) // 16) * 16
    c_pad = jnp.zeros((rows, d), F32).at[:bp].set(c_prompt).at[bp:bp + bs].set(c_sample)
    mod = _ada(c_pad, w_ada[0], b_ada[0])
    mod3 = mod.reshape(rows * 6, 1, d)

    w_in0 = w_in[0]
    wa_pad = jnp.zeros((2, GATE_PAD, GLA_K_WIDTH), F32)
    wa_pad = wa_pad.at[0, :GATE_RANK].set(gla_wa_fwd[0]).at[1, GATE_RANK:2 * GATE_RANK].set(gla_wa_bwd[0])
    p = {
        "norm1_g": norm1_g[0].reshape(1, d),
        "norm2_g": norm2_g[0].reshape(1, d),
        "final_g": final_g.reshape(1, d),
        "w_in": _to_bf16(w_in0),
        "w_gate": jnp.pad(w_in0[:, OFF_GATE:], ((0, 0), (0, GATE_PAD - 2 * GATE_RANK))).astype(BF16),
        "w_out": w_out[0],
        "w_mlp_in": w_mlp_in[0],
        "w_mlp_out": w_mlp_out[0],
        "wa_pad": wa_pad.astype(BF16),
        "ba": jnp.stack([gla_ba_fwd[0], gla_ba_bwd[0]]).reshape(2, 1, GLA_K_WIDTH),
        "gla_norm_g": gla_norm_g[0].reshape(1, GLA_DV),
        "rel_bias": rel_bias,
        "sink": attn_sink[0].reshape(1, N_ATTN_HEADS),
        "bucket": jnp.asarray(_t5_bucket_table()),
        "tri": jnp.asarray(_gla_tri_mats()),
    }
    y_prompt, p = _trunk_group(x_prompt, mod3, 0, p)
    y_sample, _ = _trunk_group(x_sample, mod3, bp, p)
    return (y_prompt, y_sample)
```
